```python
import jax, jax.numpy as jnp
from jax import lax
import numpy as np

D_MODEL = 2048
BATCH = 4
SEQ = 2048
DEPTH = 4

CHUNK = 64
MEM_LEN = 256
Q_BLOCK = 128
N_MIXERS = 2
N_RET_LAYERS = (DEPTH + 1) // 2
N_MLA_LAYERS = DEPTH // 2

RET_HEADS = 8
RET_QK_DIM = D_MODEL // RET_HEADS
RET_V_DIM = 2 * D_MODEL // RET_HEADS
RET_QK_WIDTH = RET_HEADS * RET_QK_DIM
RET_V_WIDTH = RET_HEADS * RET_V_DIM
RET_IN_WIDTH = 2 * RET_QK_WIDTH + 2 * RET_V_WIDTH

MLA_HEADS = 16
MLA_Q_RANK = 512
MLA_KV_RANK = 512
MLA_NOPE = 128
MLA_ROPE = 64
MLA_V = 128
MLA_IN_WIDTH = MLA_Q_RANK + MLA_KV_RANK + MLA_ROPE

XA_HEADS = 4
XA_DIM = D_MODEL // XA_HEADS

D_FF = 5632
CONV_W = 3

ROPE_BASE = 10000.0
LN_EPS = 1e-5
RMS_EPS = 1e-6
NEG_INF = -1e30
DEEPNORM_ALPHA = (2 * DEPTH) ** 0.25
DEEPNORM_BETA = (8 * DEPTH) ** -0.25

kernel_name = "hybrid_retention_mla_streaming_encoder"


def layer_norm(x, g, b):
    xf = x.astype(jnp.float32)
    mu = jnp.mean(xf, -1, keepdims=True)
    var = jnp.mean(jnp.square(xf - mu), -1, keepdims=True)
    y = (xf - mu) * lax.rsqrt(var + LN_EPS)
    return (y * g.astype(jnp.float32) + b.astype(jnp.float32)).astype(x.dtype)


def rms_norm(x, g):
    xf = x.astype(jnp.float32)
    y = xf * lax.rsqrt(jnp.mean(xf * xf, -1, keepdims=True) + RMS_EPS)
    return (y * g.astype(jnp.float32)).astype(x.dtype)


def head_group_norm(o, g):
    B, S, H, dv = o.shape
    of = o.astype(jnp.float32)
    mu = jnp.mean(of, -1, keepdims=True)
    var = jnp.mean(jnp.square(of - mu), -1, keepdims=True)
    y = ((of - mu) * lax.rsqrt(var + LN_EPS)).reshape(B, S, H * dv)
    return (y * g.astype(jnp.float32)).astype(o.dtype)


def rotary(x, positions):
    d = x.shape[-1]
    inv_freq = ROPE_BASE ** (-jnp.arange(0, d, 2, dtype=jnp.float32) / d)
    ang = positions.astype(jnp.float32)[..., None] * inv_freq
    cos = jnp.cos(ang)[:, :, None, :]
    sin = jnp.sin(ang)[:, :, None, :]
    xf = x.astype(jnp.float32)
    x1, x2 = xf[..., : d // 2], xf[..., d // 2:]
    out = jnp.concatenate([x1 * cos - x2 * sin, x2 * cos + x1 * sin], -1)
    return out.astype(x.dtype)


def retention_mixer(x, positions, w_in, gn_g, w_out):
    B, S, _ = x.shape
    nc = S // CHUNK
    dt = x.dtype
    proj = x @ w_in
    q, k, v, g = jnp.split(proj, [RET_QK_WIDTH, 2 * RET_QK_WIDTH,
                                  2 * RET_QK_WIDTH + RET_V_WIDTH], axis=-1)
    q = rotary(q.reshape(B, S, RET_HEADS, RET_QK_DIM), positions)
    k = rotary(k.reshape(B, S, RET_HEADS, RET_QK_DIM), positions) * (RET_QK_DIM ** -0.5)
    v = v.reshape(B, S, RET_HEADS, RET_V_DIM)

    log_gamma = jnp.log(1.0 - 2.0 ** (-5.0 - jnp.arange(RET_HEADS, dtype=jnp.float32)))
    idx = jnp.arange(CHUNK, dtype=jnp.float32)
    intra = jnp.exp(log_gamma[:, None, None] * jnp.abs(idx[:, None] - idx[None, :])).astype(dt)
    q_decay = jnp.exp(idx[:, None] * log_gamma[None, :]).astype(dt)
    k_decay = jnp.exp((CHUNK - idx)[:, None] * log_gamma[None, :]).astype(dt)
    chunk_decay = jnp.exp(CHUNK * log_gamma).astype(dt)

    def to_chunks(t):
        return jnp.moveaxis(t.reshape(B, nc, CHUNK, RET_HEADS, t.shape[-1]), 1, 0)

    def step(state, qkv):
        qc, kc, vc = qkv
        scores = jnp.einsum('bihd,bjhd->bhij', qc, kc) * intra
        o_intra = jnp.einsum('bhij,bjhe->bihe', scores, vc)
        o_cross = jnp.einsum('bihd,bhde->bihe', qc, state) * q_decay[None, :, :, None]
        new_state = (state * chunk_decay[None, :, None, None]
                     + jnp.einsum('bjhd,bjhe->bhde', kc * k_decay[None, :, :, None], vc))
        return new_state, o_intra + o_cross

    state0 = jnp.zeros((B, RET_HEADS, RET_QK_DIM, RET_V_DIM), dt)
    _, o = lax.scan(step, state0, (to_chunks(q), to_chunks(k), to_chunks(v)))
    o = jnp.moveaxis(o, 0, 1).reshape(B, S, RET_HEADS, RET_V_DIM)
    y = jax.nn.silu(g) * head_group_norm(o, gn_g)
    return y @ w_out


def mla_mixer(x, positions, w_in, q_norm_g, w_uq, kv_norm_g, w_ukv, w_out):
    B, S, _ = x.shape
    proj = x @ w_in
    c_q, c_kv, k_rope = jnp.split(proj, [MLA_Q_RANK, MLA_Q_RANK + MLA_KV_RANK], axis=-1)
    c_q = rms_norm(c_q, q_norm_g)
    c_kv = rms_norm(c_kv, kv_norm_g)
    q = (c_q @ w_uq).reshape(B, S, MLA_HEADS, MLA_NOPE + MLA_ROPE)
    q_nope = q[..., :MLA_NOPE]
    q_rope = rotary(q[..., MLA_NOPE:], positions)
    k_rope = rotary(k_rope[:, :, None, :], positions)[:, :, 0]
    kv = (c_kv @ w_ukv).reshape(B, S, MLA_HEADS, MLA_NOPE + MLA_V)
    k_nope, v = kv[..., :MLA_NOPE], kv[..., MLA_NOPE:]
    scale = (MLA_NOPE + MLA_ROPE) ** -0.5
    chunk_id = jnp.arange(S) // CHUNK
    outs = []
    for start in range(0, S, Q_BLOCK):
        end = start + Q_BLOCK
        s = (jnp.einsum('bqhd,bkhd->bhqk', q_nope[:, start:end], k_nope[:, :end])
             + jnp.einsum('bqhd,bkd->bhqk', q_rope[:, start:end], k_rope[:, :end]))
        s = s.astype(jnp.float32) * scale
        mask = chunk_id[start:end, None] >= chunk_id[None, :end]
        s = jnp.where(mask[None, None], s, NEG_INF)
        p = jax.nn.softmax(s, axis=-1).astype(v.dtype)
        outs.append(jnp.einsum('bhqk,bkhe->bqhe', p, v[:, :end]))
    o = jnp.concatenate(outs, axis=1).reshape(B, S, MLA_HEADS * MLA_V)
    return o @ w_out


def memory_cross_attention(x, mem, w_q, w_kv, w_out):
    B, S, _ = x.shape
    M = mem.shape[1]
    q = (x @ w_q).reshape(B, S, XA_HEADS, XA_DIM)
    k, v = jnp.split(mem @ w_kv, 2, axis=-1)
    k = k.reshape(B, M, XA_HEADS, XA_DIM)
    v = v.reshape(B, M, XA_HEADS, XA_DIM)
    s = jnp.einsum('bqhd,bkhd->bhqk', q, k).astype(jnp.float32) * (XA_DIM ** -0.5)
    p = jax.nn.softmax(s, axis=-1).astype(v.dtype)
    o = jnp.einsum('bhqk,bkhd->bqhd', p, v).reshape(B, S, XA_HEADS * XA_DIM)
    return o @ w_out


def conv_ffn(x, w_up, conv_w, conv_b, w_down):
    h = x @ w_up
    h = lax.conv_general_dilated(h, conv_w[:, None, :], window_strides=(1,),
                                 padding=[(CONV_W - 1, 0)],
                                 dimension_numbers=('NWC', 'WIO', 'NWC'),
                                 feature_group_count=2 * D_FF) + conv_b
    gate, val = jnp.split(h, 2, axis=-1)
    return (jax.nn.silu(gate) * val) @ w_down


def setup_inputs(seed: int = 0) -> dict:
    key = jax.random.key(seed)
    ks = iter(jax.random.split(key, 32))

    def w(shape, fan_in, scale=1.0):
        return jax.random.normal(next(ks), shape, jnp.float32) * (scale * fan_in ** -0.5)

    def gain(shape):
        return 1.0 + 0.01 * jax.random.normal(next(ks), shape, jnp.float32)

    def bias(shape):
        return 0.01 * jax.random.normal(next(ks), shape, jnp.float32)

    x = jax.random.normal(next(ks), (BATCH, SEQ, D_MODEL), jnp.float32)
    mem = jax.random.normal(next(ks), (BATCH, MEM_LEN, D_MODEL), jnp.float32)
    offset = jax.random.randint(next(ks), (BATCH, 1), 0, 4096, jnp.int32)
    positions = offset + jnp.arange(SEQ, dtype=jnp.int32)[None, :]
    return {
        "x": x,
        "mem": mem,
        "positions": positions,
        "ret_w_in": w((N_RET_LAYERS, D_MODEL, RET_IN_WIDTH), D_MODEL),
        "ret_gn_g": gain((N_RET_LAYERS, RET_V_WIDTH)),
        "ret_w_out": w((N_RET_LAYERS, RET_V_WIDTH, D_MODEL), RET_V_WIDTH, DEEPNORM_BETA),
        "mla_w_in": w((N_MLA_LAYERS, D_MODEL, MLA_IN_WIDTH), D_MODEL),
        "mla_q_norm_g": gain((N_MLA_LAYERS, MLA_Q_RANK)),
        "mla_w_uq": w((N_MLA_LAYERS, MLA_Q_RANK, MLA_HEADS * (MLA_NOPE + MLA_ROPE)), MLA_Q_RANK),
        "mla_kv_norm_g": gain((N_MLA_LAYERS, MLA_KV_RANK)),
        "mla_w_ukv": w((N_MLA_LAYERS, MLA_KV_RANK, MLA_HEADS * (MLA_NOPE + MLA_V)), MLA_KV_RANK),
        "mla_w_out": w((N_MLA_LAYERS, MLA_HEADS * MLA_V, D_MODEL), MLA_HEADS * MLA_V, DEEPNORM_BETA),
        "xa_w_q": w((DEPTH, D_MODEL, XA_HEADS * XA_DIM), D_MODEL),
        "xa_w_kv": w((DEPTH, D_MODEL, 2 * XA_HEADS * XA_DIM), D_MODEL),
        "xa_w_out": w((DEPTH, XA_HEADS * XA_DIM, D_MODEL), XA_HEADS * XA_DIM, DEEPNORM_BETA),
        "ffn_w_up": w((DEPTH, D_MODEL, 2 * D_FF), D_MODEL),
        "ffn_conv_w": w((DEPTH, CONV_W, 2 * D_FF), CONV_W),
        "ffn_conv_b": bias((DEPTH, 2 * D_FF)),
        "ffn_w_down": w((DEPTH, D_FF, D_MODEL), D_FF, DEEPNORM_BETA),
        "ln_mix_g": gain((DEPTH, D_MODEL)),
        "ln_mix_b": bias((DEPTH, D_MODEL)),
        "ln_mem_g": gain((DEPTH, D_MODEL)),
        "ln_mem_b": bias((DEPTH, D_MODEL)),
        "ln_ffn_g": gain((DEPTH, D_MODEL)),
        "ln_ffn_b": bias((DEPTH, D_MODEL)),
    }


def reference(x, mem, positions, ret_w_in, ret_gn_g, ret_w_out, mla_w_in, mla_q_norm_g,
              mla_w_uq, mla_kv_norm_g, mla_w_ukv, mla_w_out, xa_w_q, xa_w_kv, xa_w_out,
              ffn_w_up, ffn_conv_w, ffn_conv_b, ffn_w_down, ln_mix_g, ln_mix_b,
              ln_mem_g, ln_mem_b, ln_ffn_g, ln_ffn_b):
    h = x
    for layer in range(DEPTH):
        j = layer // N_MIXERS
        if layer % N_MIXERS == 0:
            mix = retention_mixer(h, positions, ret_w_in[j], ret_gn_g[j], ret_w_out[j])
        else:
            mix = mla_mixer(h, positions, mla_w_in[j], mla_q_norm_g[j], mla_w_uq[j],
                            mla_kv_norm_g[j], mla_w_ukv[j], mla_w_out[j])
        h = layer_norm(DEEPNORM_ALPHA * h + mix, ln_mix_g[layer], ln_mix_b[layer])
        h = layer_norm(DEEPNORM_ALPHA * h
                       + memory_cross_attention(h, mem, xa_w_q[layer], xa_w_kv[layer], xa_w_out[layer]),
                       ln_mem_g[layer], ln_mem_b[layer])
        h = layer_norm(DEEPNORM_ALPHA * h
                       + conv_ffn(h, ffn_w_up[layer], ffn_conv_w[layer], ffn_conv_b[layer], ffn_w_down[layer]),
                       ln_ffn_g[layer], ln_ffn_b[layer])
    return h
```

```python
import functools

import jax
import jax.numpy as jnp
from jax import lax
from jax.experimental import pallas as pl
from jax.experimental.pallas import tpu as pltpu

D_MODEL = 2048
BATCH = 4
SEQ = 2048
DEPTH = 4
CHUNK = 64
MEM_LEN = 256
N_MIXERS = 2

RET_HEADS = 8
RET_QK_DIM = D_MODEL // RET_HEADS
RET_V_DIM = 2 * D_MODEL // RET_HEADS
RET_QK_WIDTH = RET_HEADS * RET_QK_DIM
RET_V_WIDTH = RET_HEADS * RET_V_DIM
RET_IN_WIDTH = 2 * RET_QK_WIDTH + 2 * RET_V_WIDTH

MLA_HEADS = 16
MLA_Q_RANK = 512
MLA_KV_RANK = 512
MLA_NOPE = 128
MLA_ROPE = 64
MLA_V = 128

XA_HEADS = 4
XA_DIM = D_MODEL // XA_HEADS

D_FF = 5632
CONV_W = 3

ROPE_BASE = 10000.0
LN_EPS = 1e-5
RMS_EPS = 1e-6
NEG_INF = -1e30
DEEPNORM_ALPHA = (2 * DEPTH) ** 0.25

TOKENS = BATCH * SEQ
LANES = 128
BF16_SUBLANES = 16
MLA_QK_PAD = 256
RET_BLOCK = 256
VMEM_LIMIT = 56 * 1024 * 1024

F32 = jnp.float32
BF16 = jnp.bfloat16


def _params(*semantics):
    return pltpu.CompilerParams(dimension_semantics=semantics, vmem_limit_bytes=VMEM_LIMIT)


def _resident(shape, index_map):
    return pl.BlockSpec(shape, index_map, pipeline_mode=pl.Buffered(1))


def _rope_tables_kernel(pos_ref, invf_ret_ref, invf_mla_ref, cos_r, sin_r, c_m, sp_m, sn_m):
    pos = pos_ref[...].astype(F32)
    ang = pos * invf_ret_ref[...]
    cos_r[...] = jnp.cos(ang)
    sin_r[...] = jnp.sin(ang)
    angm = pos * invf_mla_ref[...]
    lane = lax.broadcasted_iota(jnp.int32, angm.shape, 1)
    half = MLA_ROPE // 2
    c = jnp.cos(angm)
    s = jnp.sin(angm)
    c_m[...] = jnp.where(lane < MLA_ROPE, c, 0.0)
    sp_m[...] = jnp.where((lane >= half) & (lane < MLA_ROPE), s, 0.0)
    sn_m[...] = jnp.where(lane < half, -s, 0.0)


def _rope_tables(positions):
    tm = 1024
    pos = positions.reshape(TOKENS, 1)
    invf_ret = ROPE_BASE ** (-jnp.arange(0, RET_QK_DIM, 2, dtype=F32) / RET_QK_DIM)
    invf_mla = ROPE_BASE ** (-jnp.arange(0, MLA_ROPE, 2, dtype=F32) / MLA_ROPE)
    invf_mla = jnp.concatenate([invf_mla, invf_mla, jnp.zeros((LANES - MLA_ROPE,), F32)])
    row = pl.BlockSpec((tm, LANES), lambda i: (i, 0))
    const = pl.BlockSpec((1, LANES), lambda i: (0, 0))
    return pl.pallas_call(
        _rope_tables_kernel,
        grid=(TOKENS // tm,),
        in_specs=[pl.BlockSpec((tm, 1), lambda i: (i, 0)), const, const],
        out_specs=[row] * 5,
        out_shape=[jax.ShapeDtypeStruct((TOKENS, LANES), F32)] * 5,
        compiler_params=_params("arbitrary"),
        name="rope_tables",
    )(pos, invf_ret.reshape(1, LANES), invf_mla.reshape(1, LANES))


def _rope_mla(x, c, sp, sn):
    half = MLA_ROPE // 2
    return x * c + pltpu.roll(x, half, 1) * sp + pltpu.roll(x, LANES - half, 1) * sn


def _matmul_kernel(a_ref, w_ref, o_ref):
    o_ref[...] = jnp.dot(a_ref[...], w_ref[...], preferred_element_type=F32).astype(o_ref.dtype)


def _matmul(a, w, tm, tn, name):
    m, k = a.shape
    n = w.shape[1]
    return pl.pallas_call(
        _matmul_kernel,
        grid=(m // tm, n // tn),
        in_specs=[pl.BlockSpec((tm, k), lambda i, j: (i, 0)),
                  pl.BlockSpec((k, tn), lambda i, j: (0, j))],
        out_specs=pl.BlockSpec((tm, tn), lambda i, j: (i, j)),
        out_shape=jax.ShapeDtypeStruct((m, n), BF16),
        compiler_params=_params("arbitrary", "arbitrary"),
        name=name,
    )(a, w)


def _mm_res_ln_kernel(nk, a_ref, w_ref, h_ref, g_ref, b_ref, of_ref, ob_ref, acc_ref):
    k = pl.program_id(1)

    @pl.when(k == 0)
    def _():
        acc_ref[...] = jnp.zeros_like(acc_ref)

    acc_ref[...] += jnp.dot(a_ref[...], w_ref[...], preferred_element_type=F32)

    @pl.when(k == nk - 1)
    def _():
        y = DEEPNORM_ALPHA * h_ref[...] + acc_ref[...]
        mu = jnp.mean(y, -1, keepdims=True)
        d = y - mu
        var = jnp.mean(d * d, -1, keepdims=True)
        out = d * lax.rsqrt(var + LN_EPS) * g_ref[...] + b_ref[...]
        of_ref[...] = out
        ob_ref[...] = out.astype(BF16)


def _mm_res_ln(a, w, h, g, b, tm, tk, name):
    m, k = a.shape
    n = w.shape[1]
    nk = k // tk
    row = pl.BlockSpec((tm, n), lambda i, kk: (i, 0))
    vec = pl.BlockSpec((1, n), lambda i, kk: (0, 0))
    return pl.pallas_call(
        functools.partial(_mm_res_ln_kernel, nk),
        grid=(m // tm, nk),
        in_specs=[pl.BlockSpec((tm, tk), lambda i, kk: (i, kk)),
                  pl.BlockSpec((tk, n), lambda i, kk: (kk, 0)),
                  row, vec, vec],
        out_specs=[row, row],
        out_shape=[jax.ShapeDtypeStruct((m, n), F32), jax.ShapeDtypeStruct((m, n), BF16)],
        scratch_shapes=[pltpu.VMEM((tm, n), F32)],
        compiler_params=_params("arbitrary", "arbitrary"),
        name=name,
    )(a, w, h, g.reshape(1, n), b.reshape(1, n))


def _ret_proj_kernel(tn, x_ref, w_ref, cos_ref, sin_ref, o_ref):
    j = pl.program_id(1)
    acc = jnp.dot(x_ref[...], w_ref[...], preferred_element_type=F32)
    n_qk = 2 * RET_QK_WIDTH // tn
    half = RET_QK_DIM // 2

    @pl.when(j < n_qk)
    def _():
        scale = jnp.where(j < n_qk // 2, 1.0, RET_QK_DIM ** -0.5).astype(F32)
        cos = cos_ref[...]
        sin = sin_ref[...]
        for hh in range(tn // RET_QK_DIM):
            lo = hh * RET_QK_DIM
            x1 = acc[:, lo:lo + half]
            x2 = acc[:, lo + half:lo + RET_QK_DIM]
            o_ref[:, lo:lo + half] = ((x1 * cos - x2 * sin) * scale).astype(o_ref.dtype)
            o_ref[:, lo + half:lo + RET_QK_DIM] = ((x2 * cos + x1 * sin) * scale).astype(o_ref.dtype)

    @pl.when(j >= n_qk)
    def _():
        o_ref[...] = acc.astype(o_ref.dtype)


def _ret_proj(xb, w, cos, sin):
    tm, tn = 1024, 512
    return pl.pallas_call(
        functools.partial(_ret_proj_kernel, tn),
        grid=(TOKENS // tm, RET_IN_WIDTH // tn),
        in_specs=[pl.BlockSpec((tm, D_MODEL), lambda i, j: (i, 0)),
                  pl.BlockSpec((D_MODEL, tn), lambda i, j: (0, j)),
                  pl.BlockSpec((tm, LANES), lambda i, j: (i, 0)),
                  pl.BlockSpec((tm, LANES), lambda i, j: (i, 0))],
        out_specs=pl.BlockSpec((tm, tn), lambda i, j: (i, j)),
        out_shape=jax.ShapeDtypeStruct((TOKENS, RET_IN_WIDTH), BF16),
        compiler_params=_params("arbitrary", "arbitrary"),
        name="ret_proj",
    )(xb, w, cos, sin)


def _ret_core_kernel(lg_ref, q_ref, k_ref, v_ref, g_ref, gn_ref, y_ref, state_ref):
    L = RET_BLOCK
    head = pl.program_id(1)
    blk = pl.program_id(2)

    @pl.when(blk == 0)
    def _():
        state_ref[...] = jnp.zeros_like(state_ref)

    lg = lg_ref[head]
    q = q_ref[...]
    k = k_ref[...]
    v = v_ref[...]

    n = lax.broadcasted_iota(jnp.int32, (L, L), 0)
    m = lax.broadcasted_iota(jnp.int32, (L, L), 1)
    dist = jnp.abs(n - m).astype(F32)
    visible = (m // CHUNK) <= (n // CHUNK)
    decay = jnp.where(visible, jnp.exp(lg * dist), 0.0)

    s = lax.dot_general(q, k, (((1,), (1,)), ((), ())), preferred_element_type=F32)
    o = jnp.dot((s * decay).astype(BF16), v, preferred_element_type=F32)

    idx = lax.broadcasted_iota(jnp.int32, (L, 1), 0).astype(F32)
    state = state_ref[...]
    o = o + jnp.dot(q, state.astype(BF16), preferred_element_type=F32) * jnp.exp(idx * lg)

    k_dec = (k.astype(F32) * jnp.exp((L - idx) * lg)).astype(BF16)
    upd = lax.dot_general(k_dec, v, (((0,), (0,)), ((), ())), preferred_element_type=F32)
    block_decay = jnp.exp(jnp.full((1, RET_V_DIM), L, F32) * lg)
    state_ref[...] = state * block_decay + upd

    mu = jnp.mean(o, -1, keepdims=True)
    d = o - mu
    var = jnp.mean(d * d, -1, keepdims=True)
    normed = d * lax.rsqrt(var + LN_EPS) * gn_ref[...]
    gate = g_ref[...].astype(F32)
    y_ref[...] = (gate * jax.nn.sigmoid(gate) * normed).astype(y_ref.dtype)


def _ret_core(proj, gn_g):
    L = RET_BLOCK
    nb = SEQ // L
    log_gamma = jnp.log(1.0 - 2.0 ** (-5.0 - jnp.arange(RET_HEADS, dtype=F32)))
    k_off = RET_QK_WIDTH // RET_QK_DIM
    v_off = 2 * RET_QK_WIDTH // RET_V_DIM
    g_off = v_off + RET_HEADS
    return pl.pallas_call(
        _ret_core_kernel,
        grid=(BATCH, RET_HEADS, nb),
        in_specs=[pl.BlockSpec(memory_space=pltpu.SMEM),
                  pl.BlockSpec((L, RET_QK_DIM), lambda b, h, c: (b * nb + c, h)),
                  pl.BlockSpec((L, RET_QK_DIM), lambda b, h, c: (b * nb + c, k_off + h)),
                  pl.BlockSpec((L, RET_V_DIM), lambda b, h, c: (b * nb + c, v_off + h)),
                  pl.BlockSpec((L, RET_V_DIM), lambda b, h, c: (b * nb + c, g_off + h)),
                  pl.BlockSpec((1, RET_V_DIM), lambda b, h, c: (0, h))],
        out_specs=pl.BlockSpec((L, RET_V_DIM), lambda b, h, c: (b * nb + c, h)),
        out_shape=jax.ShapeDtypeStruct((TOKENS, RET_V_WIDTH), BF16),
        scratch_shapes=[pltpu.VMEM((RET_QK_DIM, RET_V_DIM), F32)],
        compiler_params=_params("arbitrary", "arbitrary", "arbitrary"),
        name="ret_core",
    )(log_gamma, proj, proj, proj, proj, gn_g.reshape(1, RET_V_WIDTH))


def _rms(x, g):
    return x * lax.rsqrt(jnp.mean(x * x, -1, keepdims=True) + RMS_EPS) * g


def _mla_proj_kernel(x_ref, w_in_ref, qg_ref, kvg_ref, w_uq_ref, w_uk_ref, w_uv_ref,
                     c_ref, sp_ref, sn_ref, q_ref, k_ref, v_ref):
    c = c_ref[...]
    sp = sp_ref[...]
    sn = sn_ref[...]
    proj = jnp.dot(x_ref[...], w_in_ref[...], preferred_element_type=F32)
    c_q = _rms(proj[:, :MLA_Q_RANK], qg_ref[...]).astype(BF16)
    c_kv = _rms(proj[:, MLA_Q_RANK:MLA_Q_RANK + MLA_KV_RANK], kvg_ref[...]).astype(BF16)
    k_rope = _rope_mla(proj[:, MLA_Q_RANK + MLA_KV_RANK:], c, sp, sn).astype(BF16)

    q = jnp.dot(c_q, w_uq_ref[...], preferred_element_type=F32)
    k_nope = jnp.dot(c_kv, w_uk_ref[...], preferred_element_type=F32)
    for hh in range(MLA_HEADS):
        lo = hh * MLA_QK_PAD
        q_ref[:, lo:lo + MLA_NOPE] = q[:, lo:lo + MLA_NOPE].astype(BF16)
        q_ref[:, lo + MLA_NOPE:lo + MLA_QK_PAD] = _rope_mla(
            q[:, lo + MLA_NOPE:lo + MLA_QK_PAD], c, sp, sn).astype(BF16)
        k_ref[:, lo:lo + MLA_NOPE] = k_nope[:, hh * MLA_NOPE:(hh + 1) * MLA_NOPE].astype(BF16)
        k_ref[:, lo + MLA_NOPE:lo + MLA_QK_PAD] = k_rope
    v_ref[...] = jnp.dot(c_kv, w_uv_ref[...], preferred_element_type=F32).astype(BF16)


def _mla_proj(xb, w_in, qg, kvg, w_uq, w_uk, w_uv, c, sp, sn):
    tm = 256
    qk_w = MLA_HEADS * MLA_QK_PAD
    v_w = MLA_HEADS * MLA_V
    row = lambda w: pl.BlockSpec((tm, w), lambda i: (i, 0))
    res = lambda a: _resident(a.shape, lambda i: (0, 0))
    return pl.pallas_call(
        _mla_proj_kernel,
        grid=(TOKENS // tm,),
        in_specs=[row(D_MODEL), res(w_in), res(qg), res(kvg), res(w_uq), res(w_uk), res(w_uv),
                  row(LANES), row(LANES), row(LANES)],
        out_specs=[row(qk_w), row(qk_w), row(v_w)],
        out_shape=[jax.ShapeDtypeStruct((TOKENS, qk_w), BF16),
                   jax.ShapeDtypeStruct((TOKENS, qk_w), BF16),
                   jax.ShapeDtypeStruct((TOKENS, v_w), BF16)],
        compiler_params=_params("arbitrary"),
        name="mla_proj",
    )(xb, w_in, qg, kvg, w_uq, w_uk, w_uv, c, sp, sn)


def _mla_attn_kernel(tq, q_ref, k_ref, v_ref, o_ref):
    i = pl.program_id(2)
    scale = (MLA_NOPE + MLA_ROPE) ** -0.5
    q = q_ref[...]

    def scores(kb):
        k = k_ref[pl.ds(pl.multiple_of(kb * tq, tq), tq), :]
        return lax.dot_general(q, k, (((1,), (1,)), ((), ())), preferred_element_type=F32) * scale

    def update(kb, s, carry):
        m_prev, l_prev, acc = carry
        m_new = jnp.maximum(m_prev, jnp.max(s, -1, keepdims=True))
        alpha = jnp.exp(m_prev - m_new)
        p = jnp.exp(s - m_new)
        v = v_ref[pl.ds(pl.multiple_of(kb * tq, tq), tq), :]
        acc = alpha * acc + jnp.dot(p.astype(BF16), v, preferred_element_type=F32)
        return m_new, alpha * l_prev + jnp.sum(p, -1, keepdims=True), acc

    init = (jnp.full((tq, 1), NEG_INF, F32), jnp.zeros((tq, 1), F32), jnp.zeros((tq, MLA_V), F32))
    carry = lax.fori_loop(0, i, lambda kb, cr: update(kb, scores(kb), cr), init)

    n = lax.broadcasted_iota(jnp.int32, (tq, tq), 0)
    m = lax.broadcasted_iota(jnp.int32, (tq, tq), 1)
    s = jnp.where((n // CHUNK) >= (m // CHUNK), scores(i), NEG_INF)
    _, l, acc = update(i, s, carry)
    o_ref[...] = (acc / l).astype(o_ref.dtype)


def _mla_attn(q, k, v):
    tq = 256
    nq = SEQ // tq
    return pl.pallas_call(
        functools.partial(_mla_attn_kernel, tq),
        grid=(BATCH, MLA_HEADS, nq),
        in_specs=[pl.BlockSpec((tq, MLA_QK_PAD), lambda b, h, i: (b * nq + i, h)),
                  pl.BlockSpec((SEQ, MLA_QK_PAD), lambda b, h, i: (b, h)),
                  pl.BlockSpec((SEQ, MLA_V), lambda b, h, i: (b, h))],
        out_specs=pl.BlockSpec((tq, MLA_V), lambda b, h, i: (b * nq + i, h)),
        out_shape=jax.ShapeDtypeStruct((TOKENS, MLA_HEADS * MLA_V), BF16),
        compiler_params=_params("arbitrary", "arbitrary", "arbitrary"),
        name="mla_attn",
    )(q, k, v)


def _xa_attn_kernel(q_ref, k_ref, v_ref, o_ref):
    s = lax.dot_general(q_ref[...], k_ref[...], (((1,), (1,)), ((), ())),
                        preferred_element_type=F32) * (XA_DIM ** -0.5)
    e = jnp.exp(s - jnp.max(s, -1, keepdims=True))
    p = e / jnp.sum(e, -1, keepdims=True)
    o_ref[...] = jnp.dot(p.astype(BF16), v_ref[...], preferred_element_type=F32).astype(o_ref.dtype)


def _xa_attn(q, kv):
    tq = 512
    nq = SEQ // tq
    return pl.pallas_call(
        _xa_attn_kernel,
        grid=(BATCH, XA_HEADS, nq),
        in_specs=[pl.BlockSpec((tq, XA_DIM), lambda b, h, i: (b * nq + i, h)),
                  pl.BlockSpec((MEM_LEN, XA_DIM), lambda b, h, i: (b, h)),
                  pl.BlockSpec((MEM_LEN, XA_DIM), lambda b, h, i: (b, XA_HEADS + h))],
        out_specs=pl.BlockSpec((tq, XA_DIM), lambda b, h, i: (b * nq + i, h)),
        out_shape=jax.ShapeDtypeStruct((TOKENS, D_MODEL), BF16),
        compiler_params=_params("arbitrary", "arbitrary", "arbitrary"),
        name="xa_attn",
    )(q, kv, kv)


def _ffn_up_kernel(tm, x_ref, halo_ref, wg_ref, wv_ref, cwg_ref, cwv_ref, cbg_ref, cbv_ref, o_ref, xs_ref):
    i = pl.program_id(0)
    j = pl.program_id(1)
    halo = BF16_SUBLANES

    @pl.when(j == 0)
    def _():
        seq_start = (i * tm) % SEQ == 0
        xs_ref[0:halo, :] = jnp.where(seq_start, jnp.zeros_like(halo_ref), halo_ref[...])
        xs_ref[halo:, :] = x_ref[...]

    xs = xs_ref[...]

    def conv(w_ref, cw_ref, cb_ref):
        h = jnp.dot(xs, w_ref[...], preferred_element_type=F32)
        cw = cw_ref[...]
        out = cw[2:3] * h + cw[1:2] * pltpu.roll(h, 1, 0) + cw[0:1] * pltpu.roll(h, 2, 0)
        return out[halo:] + cb_ref[...]

    gate = conv(wg_ref, cwg_ref, cbg_ref)
    val = conv(wv_ref, cwv_ref, cbv_ref)
    o_ref[...] = (gate * jax.nn.sigmoid(gate) * val).astype(o_ref.dtype)


def _ffn_up(xb, w_up, conv_w, conv_b):
    tm, tn = 512, 512
    halo = BF16_SUBLANES
    nj = D_FF // tn
    halo_blocks = tm // halo
    conv_b = conv_b.reshape(1, 2 * D_FF)
    return pl.pallas_call(
        functools.partial(_ffn_up_kernel, tm),
        grid=(TOKENS // tm, nj),
        in_specs=[pl.BlockSpec((tm, D_MODEL), lambda i, j: (i, 0)),
                  pl.BlockSpec((halo, D_MODEL), lambda i, j: (jnp.maximum(i * halo_blocks - 1, 0), 0)),
                  pl.BlockSpec((D_MODEL, tn), lambda i, j: (0, j)),
                  pl.BlockSpec((D_MODEL, tn), lambda i, j: (0, nj + j)),
                  pl.BlockSpec((CONV_W, tn), lambda i, j: (0, j)),
                  pl.BlockSpec((CONV_W, tn), lambda i, j: (0, nj + j)),
                  pl.BlockSpec((1, tn), lambda i, j: (0, j)),
                  pl.BlockSpec((1, tn), lambda i, j: (0, nj + j))],
        out_specs=pl.BlockSpec((tm, tn), lambda i, j: (i, j)),
        out_shape=jax.ShapeDtypeStruct((TOKENS, D_FF), BF16),
        scratch_shapes=[pltpu.VMEM((halo + tm, D_MODEL), BF16)],
        compiler_params=_params("arbitrary", "arbitrary"),
        name="ffn_up",
    )(xb, xb, w_up, w_up, conv_w, conv_w, conv_b, conv_b)


def _mla_weights(w_in, w_uq, w_ukv):
    pad = LANES - MLA_ROPE
    w_in = jnp.pad(w_in, ((0, 0), (0, pad)))
    w_uq = w_uq.reshape(MLA_Q_RANK, MLA_HEADS, MLA_NOPE + MLA_ROPE)
    w_uq = jnp.pad(w_uq, ((0, 0), (0, 0), (0, pad))).reshape(MLA_Q_RANK, MLA_HEADS * MLA_QK_PAD)
    w_ukv = w_ukv.reshape(MLA_KV_RANK, MLA_HEADS, MLA_NOPE + MLA_V)
    w_uk = w_ukv[:, :, :MLA_NOPE].reshape(MLA_KV_RANK, MLA_HEADS * MLA_NOPE)
    w_uv = w_ukv[:, :, MLA_NOPE:].reshape(MLA_KV_RANK, MLA_HEADS * MLA_V)
    return w_in.astype(BF16), w_uq.astype(BF16), w_uk.astype(BF16), w_uv.astype(BF16)


def kernel(x, mem, positions, ret_w_in, ret_gn_g, ret_w_out, mla_w_in, mla_q_norm_g, mla_w_uq, mla_kv_norm_g, mla_w_ukv, mla_w_out, xa_w_q, xa_w_kv, xa_w_out, ffn_w_up, ffn_conv_w, ffn_conv_b, ffn_w_down, ln_mix_g, ln_mix_b, ln_mem_g, ln_mem_b, ln_ffn_g, ln_ffn_b):
    h = x.reshape(TOKENS, D_MODEL)
    hb = h.astype(BF16)
    mem_b = mem.reshape(BATCH * MEM_LEN, D_MODEL).astype(BF16)
    cos_r, sin_r, c_m, sp_m, sn_m = _rope_tables(positions)

    for layer in range(DEPTH):
        j = layer // N_MIXERS
        if layer % N_MIXERS == 0:
            proj = _ret_proj(hb, ret_w_in[j].astype(BF16), cos_r, sin_r)
            mix_in = _ret_core(proj, ret_gn_g[j])
            w_out = ret_w_out[j]
        else:
            w_in, w_uq, w_uk, w_uv = _mla_weights(mla_w_in[j], mla_w_uq[j], mla_w_ukv[j])
            q, k, v = _mla_proj(hb, w_in, mla_q_norm_g[j].reshape(1, MLA_Q_RANK),
                                mla_kv_norm_g[j].reshape(1, MLA_KV_RANK), w_uq, w_uk, w_uv, c_m, sp_m, sn_m)
            mix_in = _mla_attn(q, k, v)
            w_out = mla_w_out[j]
        h, hb = _mm_res_ln(mix_in, w_out.astype(BF16), h, ln_mix_g[layer], ln_mix_b[layer],
                           tm=512, tk=1024, name="mix_out_ln")

        xq = _matmul(hb, xa_w_q[layer].astype(BF16), tm=1024, tn=512, name="xa_q")
        xkv = _matmul(mem_b, xa_w_kv[layer].astype(BF16), tm=BATCH * MEM_LEN, tn=512, name="xa_kv")
        xo = _xa_attn(xq, xkv)
        h, hb = _mm_res_ln(xo, xa_w_out[layer].astype(BF16), h, ln_mem_g[layer], ln_mem_b[layer],
                           tm=512, tk=1024, name="xa_out_ln")

        act = _ffn_up(hb, ffn_w_up[layer].astype(BF16), ffn_conv_w[layer], ffn_conv_b[layer])
        h, hb = _mm_res_ln(act, ffn_w_down[layer].astype(BF16), h, ln_ffn_g[layer], ln_ffn_b[layer],
                           tm=512, tk=1408, name="ffn_down_ln")

    return h.reshape(BATCH, SEQ, D_MODEL)
```

```python
import functools

import jax
import jax.numpy as jnp
from jax import lax
from jax.experimental import pallas as pl
from jax.experimental.pallas import tpu as pltpu

D_MODEL = 2048
BATCH = 4
SEQ = 2048
DEPTH = 4
CHUNK = 64
MEM_LEN = 256
N_MIXERS = 2

RET_HEADS = 8
RET_QK_DIM = D_MODEL // RET_HEADS
RET_V_DIM = 2 * D_MODEL // RET_HEADS
RET_QK_WIDTH = RET_HEADS * RET_QK_DIM
RET_V_WIDTH = RET_HEADS * RET_V_DIM

MLA_HEADS = 16
MLA_Q_RANK = 512
MLA_KV_RANK = 512
MLA_NOPE = 128
MLA_ROPE = 64
MLA_V = 128

XA_HEADS = 4
XA_DIM = D_MODEL // XA_HEADS

D_FF = 5632
CONV_W = 3

ROPE_BASE = 10000.0
LN_EPS = 1e-5
RMS_EPS = 1e-6
NEG_INF = -1e30
DEEPNORM_ALPHA = (2 * DEPTH) ** 0.25

TOKENS = BATCH * SEQ
LANES = 128
BF16_SUBLANES = 16
MLA_QK_PAD = 256
RET_BLOCK = 256
VMEM_LIMIT = 56 * 1024 * 1024

X_ROWS = 2 * SEQ
ROW_CHUNK = 1024
LN_SUB = 256

F32 = jnp.float32
BF16 = jnp.bfloat16


def _params(*semantics):
    return pltpu.CompilerParams(dimension_semantics=semantics, vmem_limit_bytes=VMEM_LIMIT)


def _single(shape, index_map):
    return pl.BlockSpec(shape, index_map, pipeline_mode=pl.Buffered(1))


def _rope_tables_kernel(pos_ref, invf_ret_ref, invf_mla_ref, cos_r, sin_r, c_m, sp_m, sn_m):
    pos = pos_ref[...].astype(F32)
    ang = pos * invf_ret_ref[...]
    cos_r[...] = jnp.cos(ang)
    sin_r[...] = jnp.sin(ang)
    angm = pos * invf_mla_ref[...]
    lane = lax.broadcasted_iota(jnp.int32, angm.shape, 1)
    half = MLA_ROPE // 2
    c = jnp.cos(angm)
    s = jnp.sin(angm)
    c_m[...] = jnp.where(lane < MLA_ROPE, c, 0.0)
    sp_m[...] = jnp.where((lane >= half) & (lane < MLA_ROPE), s, 0.0)
    sn_m[...] = jnp.where(lane < half, -s, 0.0)


def _rope_tables(positions):
    tm = 1024
    pos = positions.reshape(TOKENS, 1)
    invf_ret = ROPE_BASE ** (-jnp.arange(0, RET_QK_DIM, 2, dtype=F32) / RET_QK_DIM)
    invf_mla = ROPE_BASE ** (-jnp.arange(0, MLA_ROPE, 2, dtype=F32) / MLA_ROPE)
    invf_mla = jnp.concatenate([invf_mla, invf_mla, jnp.zeros((LANES - MLA_ROPE,), F32)])
    row = pl.BlockSpec((tm, LANES), lambda i: (i, 0))
    const = pl.BlockSpec((1, LANES), lambda i: (0, 0))
    return pl.pallas_call(
        _rope_tables_kernel,
        grid=(TOKENS // tm,),
        in_specs=[pl.BlockSpec((tm, 1), lambda i: (i, 0)), const, const],
        out_specs=[row] * 5,
        out_shape=[jax.ShapeDtypeStruct((TOKENS, LANES), F32)] * 5,
        compiler_params=_params("arbitrary"),
        name="rope_tables",
    )(pos, invf_ret.reshape(1, LANES), invf_mla.reshape(1, LANES))


def _rope_mla(x, c, sp, sn):
    half = MLA_ROPE // 2
    return x * c + pltpu.roll(x, half, 1) * sp + pltpu.roll(x, LANES - half, 1) * sn


def _row_chunks(rows):
    chunk = min(ROW_CHUNK, rows)
    return [(r, chunk) for r in range(0, rows, chunk)]


def _xres_matmul_kernel(x_ref, w_ref, o_ref):
    w = w_ref[...].astype(BF16)
    for r, n in _row_chunks(x_ref.shape[0]):
        o_ref[r:r + n, :] = jnp.dot(x_ref[r:r + n, :], w, preferred_element_type=F32).astype(o_ref.dtype)


def _w_cols(w_stack, layer, col0, tn):
    k = w_stack.shape[1]
    first = col0 // tn
    return pl.BlockSpec((None, k, tn), lambda i, j: (layer, 0, first + j))


def _xres_matmul(x, w_stack, layer, col0, n, tn, name):
    m, k = x.shape
    rows = min(X_ROWS, m)
    return pl.pallas_call(
        _xres_matmul_kernel,
        grid=(m // rows, n // tn),
        in_specs=[_single((rows, k), lambda i, j: (i, 0)), _w_cols(w_stack, layer, col0, tn)],
        out_specs=pl.BlockSpec((rows, tn), lambda i, j: (i, j)),
        out_shape=jax.ShapeDtypeStruct((m, n), BF16),
        compiler_params=_params("arbitrary", "arbitrary"),
        name=name,
    )(x, w_stack)


def _ret_qk_kernel(tn, x_ref, w_ref, cos_ref, sin_ref, o_ref):
    j = pl.program_id(1)
    w = w_ref[...].astype(BF16)
    scale = jnp.where(j >= RET_QK_WIDTH // tn, RET_QK_DIM ** -0.5, 1.0).astype(F32)
    half = RET_QK_DIM // 2
    for r, n in _row_chunks(x_ref.shape[0]):
        acc = jnp.dot(x_ref[r:r + n, :], w, preferred_element_type=F32)
        cos = cos_ref[r:r + n, :]
        sin = sin_ref[r:r + n, :]
        for lo in range(0, tn, RET_QK_DIM):
            x1 = acc[:, lo:lo + half]
            x2 = acc[:, lo + half:lo + RET_QK_DIM]
            o_ref[r:r + n, lo:lo + half] = ((x1 * cos - x2 * sin) * scale).astype(o_ref.dtype)
            o_ref[r:r + n, lo + half:lo + RET_QK_DIM] = ((x2 * cos + x1 * sin) * scale).astype(o_ref.dtype)


def _ret_qk(xb, w_stack, layer, cos, sin):
    tn = 512
    n = 2 * RET_QK_WIDTH
    rows = X_ROWS
    return pl.pallas_call(
        functools.partial(_ret_qk_kernel, tn),
        grid=(TOKENS // rows, n // tn),
        in_specs=[_single((rows, D_MODEL), lambda i, j: (i, 0)),
                  _w_cols(w_stack, layer, 0, tn),
                  _single((rows, LANES), lambda i, j: (i, 0)),
                  _single((rows, LANES), lambda i, j: (i, 0))],
        out_specs=pl.BlockSpec((rows, tn), lambda i, j: (i, j)),
        out_shape=jax.ShapeDtypeStruct((TOKENS, n), BF16),
        compiler_params=_params("arbitrary", "arbitrary"),
        name="ret_qk",
    )(xb, w_stack, cos, sin)


def _ffn_up_kernel(x_ref, wg_ref, wv_ref, cwg_ref, cwv_ref, cbg_ref, cbv_ref, o_ref):
    halo = BF16_SUBLANES
    wg = wg_ref[...].astype(BF16)
    wv = wv_ref[...].astype(BF16)
    for r, n in _row_chunks(x_ref.shape[0]):
        seq_start = r % SEQ == 0
        lo = r if seq_start else r - halo
        xs = x_ref[lo:r + n, :]

        def conv(w, cw_ref, cb_ref):
            h = jnp.dot(xs, w, preferred_element_type=F32)
            h1 = pltpu.roll(h, 1, 0)
            h2 = pltpu.roll(h, 2, 0)
            cw = cw_ref[...]
            if seq_start:
                row = lax.broadcasted_iota(jnp.int32, (n, 1), 0)
                h1 = jnp.where(row >= 1, h1, 0.0)
                h2 = jnp.where(row >= 2, h2, 0.0)
                out = cw[2:3] * h + cw[1:2] * h1 + cw[0:1] * h2
            else:
                out = (cw[2:3] * h + cw[1:2] * h1 + cw[0:1] * h2)[halo:]
            return out + cb_ref[...]

        gate = conv(wg, cwg_ref, cbg_ref)
        val = conv(wv, cwv_ref, cbv_ref)
        o_ref[r:r + n, :] = (gate * jax.nn.sigmoid(gate) * val).astype(o_ref.dtype)


def _ffn_up(xb, w_up, conv_w, conv_b, layer):
    tn = 256
    nj = D_FF // tn
    rows = X_ROWS
    conv_b = conv_b.reshape(DEPTH, 1, 2 * D_FF)
    taps = lambda depth, first: pl.BlockSpec((None, depth, tn), lambda i, j: (layer, 0, first + j))
    return pl.pallas_call(
        _ffn_up_kernel,
        grid=(TOKENS // rows, nj),
        in_specs=[_single((rows, D_MODEL), lambda i, j: (i, 0)),
                  _w_cols(w_up, layer, 0, tn), _w_cols(w_up, layer, D_FF, tn),
                  taps(CONV_W, 0), taps(CONV_W, nj), taps(1, 0), taps(1, nj)],
        out_specs=pl.BlockSpec((rows, tn), lambda i, j: (i, j)),
        out_shape=jax.ShapeDtypeStruct((TOKENS, D_FF), BF16),
        compiler_params=_params("arbitrary", "arbitrary"),
        name="ffn_up",
    )(xb, w_up, w_up, conv_w, conv_w, conv_b, conv_b)


def _mm_res_ln_kernel(a_ref, w_ref, h_ref, g_ref, b_ref, of_ref, ob_ref):
    g = g_ref[...]
    b = b_ref[...]
    for r in range(0, a_ref.shape[0], LN_SUB):
        rows = slice(r, r + LN_SUB)
        y = DEEPNORM_ALPHA * h_ref[rows, :] + jnp.dot(a_ref[rows, :], w_ref[...], preferred_element_type=F32)
        mu = jnp.mean(y, -1, keepdims=True)
        d = y - mu
        var = jnp.mean(d * d, -1, keepdims=True)
        out = d * lax.rsqrt(var + LN_EPS) * g + b
        of_ref[rows, :] = out
        ob_ref[rows, :] = out.astype(BF16)


def _mm_res_ln(a, w, h, g, b, tm, name):
    m, k = a.shape
    n = w.shape[1]
    row = lambda width: pl.BlockSpec((tm, width), lambda i: (i, 0))
    vec = pl.BlockSpec((1, n), lambda i: (0, 0))
    return pl.pallas_call(
        _mm_res_ln_kernel,
        grid=(m // tm,),
        in_specs=[row(k), _single((k, n), lambda i: (0, 0)), row(n), vec, vec],
        out_specs=[row(n), row(n)],
        out_shape=[jax.ShapeDtypeStruct((m, n), F32), jax.ShapeDtypeStruct((m, n), BF16)],
        compiler_params=_params("arbitrary"),
        name=name,
    )(a, w, h, g.reshape(1, n), b.reshape(1, n))


def _ret_core_kernel(lg_ref, q_ref, k_ref, v_ref, g_ref, gn_ref, y_ref, state_ref, decay_ref):
    L = RET_BLOCK
    head = pl.program_id(1)
    step = pl.program_id(2)
    lg = lg_ref[head]

    @pl.when(step == 0)
    def _():
        state_ref[...] = jnp.zeros_like(state_ref)
        n = lax.broadcasted_iota(jnp.int32, (L, L), 0)
        m = lax.broadcasted_iota(jnp.int32, (L, L), 1)
        dist = jnp.abs(n - m).astype(F32)
        decay_ref[...] = jnp.where((m // CHUNK) <= (n // CHUNK), jnp.exp(lg * dist), 0.0)

    idx = lax.broadcasted_iota(jnp.int32, (L, 1), 0).astype(F32)
    q_decay = jnp.exp(idx * lg)
    k_decay = jnp.exp((L - idx) * lg)
    block_decay = jnp.exp(jnp.full((1, RET_V_DIM), L, F32) * lg)
    decay = decay_ref[...]
    gn = gn_ref[...]
    state = state_ref[...]
    for r in range(0, q_ref.shape[0], L):
        rows = slice(r, r + L)
        q = q_ref[rows, :]
        k = k_ref[rows, :]
        v = v_ref[rows, :]
        s = lax.dot_general(q, k, (((1,), (1,)), ((), ())), preferred_element_type=F32)
        o = jnp.dot((s * decay).astype(BF16), v, preferred_element_type=F32)
        o = o + jnp.dot(q, state.astype(BF16), preferred_element_type=F32) * q_decay
        k_dec = (k.astype(F32) * k_decay).astype(BF16)
        upd = lax.dot_general(k_dec, v, (((0,), (0,)), ((), ())), preferred_element_type=F32)
        state = state * block_decay + upd

        mu = jnp.mean(o, -1, keepdims=True)
        d = o - mu
        var = jnp.mean(d * d, -1, keepdims=True)
        normed = d * lax.rsqrt(var + LN_EPS) * gn
        gate = g_ref[rows, :].astype(F32)
        y_ref[rows, :] = (gate * jax.nn.sigmoid(gate) * normed).astype(y_ref.dtype)
    state_ref[...] = state


def _ret_core(qk, vg, gn_g):
    ts = 1024
    ns = SEQ // ts
    log_gamma = jnp.log(1.0 - 2.0 ** (-5.0 - jnp.arange(RET_HEADS, dtype=F32)))
    return pl.pallas_call(
        _ret_core_kernel,
        grid=(BATCH, RET_HEADS, ns),
        in_specs=[pl.BlockSpec(memory_space=pltpu.SMEM),
                  pl.BlockSpec((ts, RET_QK_DIM), lambda b, h, c: (b * ns + c, h)),
                  pl.BlockSpec((ts, RET_QK_DIM), lambda b, h, c: (b * ns + c, RET_HEADS + h)),
                  pl.BlockSpec((ts, RET_V_DIM), lambda b, h, c: (b * ns + c, h)),
                  pl.BlockSpec((ts, RET_V_DIM), lambda b, h, c: (b * ns + c, RET_HEADS + h)),
                  pl.BlockSpec((1, RET_V_DIM), lambda b, h, c: (0, h))],
        out_specs=pl.BlockSpec((ts, RET_V_DIM), lambda b, h, c: (b * ns + c, h)),
        out_shape=jax.ShapeDtypeStruct((TOKENS, RET_V_WIDTH), BF16),
        scratch_shapes=[pltpu.VMEM((RET_QK_DIM, RET_V_DIM), F32),
                        pltpu.VMEM((RET_BLOCK, RET_BLOCK), F32)],
        compiler_params=_params("arbitrary", "arbitrary", "arbitrary"),
        name="ret_core",
    )(log_gamma, qk, qk, vg, vg, gn_g.reshape(1, RET_V_WIDTH))


def _rms(x, g):
    return x * lax.rsqrt(jnp.mean(x * x, -1, keepdims=True) + RMS_EPS) * g


def _mla_proj_kernel(x_ref, w_in_ref, qg_ref, kvg_ref, w_uq_ref, w_uk_ref, w_uv_ref,
                     c_ref, sp_ref, sn_ref, q_ref, k_ref, v_ref):
    c = c_ref[...]
    sp = sp_ref[...]
    sn = sn_ref[...]
    proj = jnp.dot(x_ref[...], w_in_ref[...], preferred_element_type=F32)
    c_q = _rms(proj[:, :MLA_Q_RANK], qg_ref[...]).astype(BF16)
    c_kv = _rms(proj[:, MLA_Q_RANK:MLA_Q_RANK + MLA_KV_RANK], kvg_ref[...]).astype(BF16)
    k_rope = _rope_mla(proj[:, MLA_Q_RANK + MLA_KV_RANK:], c, sp, sn).astype(BF16)

    q = jnp.dot(c_q, w_uq_ref[...], preferred_element_type=F32)
    k_nope = jnp.dot(c_kv, w_uk_ref[...], preferred_element_type=F32)
    for hh in range(MLA_HEADS):
        lo = hh * MLA_QK_PAD
        q_ref[:, lo:lo + MLA_NOPE] = q[:, lo:lo + MLA_NOPE].astype(BF16)
        q_ref[:, lo + MLA_NOPE:lo + MLA_QK_PAD] = _rope_mla(
            q[:, lo + MLA_NOPE:lo + MLA_QK_PAD], c, sp, sn).astype(BF16)
        k_ref[:, lo:lo + MLA_NOPE] = k_nope[:, hh * MLA_NOPE:(hh + 1) * MLA_NOPE].astype(BF16)
        k_ref[:, lo + MLA_NOPE:lo + MLA_QK_PAD] = k_rope
    v_ref[...] = jnp.dot(c_kv, w_uv_ref[...], preferred_element_type=F32).astype(BF16)


def _mla_proj(xb, w_in, qg, kvg, w_uq, w_uk, w_uv, c, sp, sn):
    tm = 256
    qk_w = MLA_HEADS * MLA_QK_PAD
    v_w = MLA_HEADS * MLA_V
    row = lambda w: pl.BlockSpec((tm, w), lambda i: (i, 0))
    res = lambda a: _single(a.shape, lambda i: (0, 0))
    return pl.pallas_call(
        _mla_proj_kernel,
        grid=(TOKENS // tm,),
        in_specs=[row(D_MODEL), res(w_in), res(qg), res(kvg), res(w_uq), res(w_uk), res(w_uv),
                  row(LANES), row(LANES), row(LANES)],
        out_specs=[row(qk_w), row(qk_w), row(v_w)],
        out_shape=[jax.ShapeDtypeStruct((TOKENS, qk_w), BF16),
                   jax.ShapeDtypeStruct((TOKENS, qk_w), BF16),
                   jax.ShapeDtypeStruct((TOKENS, v_w), BF16)],
        compiler_params=_params("arbitrary"),
        name="mla_proj",
    )(xb, w_in, qg, kvg, w_uq, w_uk, w_uv, c, sp, sn)


def _mla_attn_kernel(tq, q_ref, k_ref, v_ref, o_ref):
    i = pl.program_id(2)
    scale = (MLA_NOPE + MLA_ROPE) ** -0.5
    dims = (((1,), (1,)), ((), ()))
    n = lax.broadcasted_iota(jnp.int32, (tq, tq), 0)
    m = lax.broadcasted_iota(jnp.int32, (tq, tq), 1)
    visible = (n // CHUNK) >= (m // CHUNK)

    for c in range(SEQ // tq):
        @pl.when(i == c)
        def _(c=c):
            lo = c * tq
            q = q_ref[...]
            s_d = lax.dot_general(q, k_ref[lo:lo + tq, :], dims, preferred_element_type=F32) * scale
            s_d = jnp.where(visible, s_d, NEG_INF)
            mx = jnp.max(s_d, -1, keepdims=True)
            if c:
                s_p = lax.dot_general(q, k_ref[0:lo, :], dims, preferred_element_type=F32) * scale
                mx = jnp.maximum(mx, jnp.max(s_p, -1, keepdims=True))
            p_d = jnp.exp(s_d - mx)
            l = jnp.sum(p_d, -1, keepdims=True)
            acc = jnp.dot(p_d.astype(BF16), v_ref[lo:lo + tq, :], preferred_element_type=F32)
            if c:
                p_p = jnp.exp(s_p - mx)
                l = l + jnp.sum(p_p, -1, keepdims=True)
                acc = acc + jnp.dot(p_p.astype(BF16), v_ref[0:lo, :], preferred_element_type=F32)
            o_ref[...] = (acc * (1.0 / l)).astype(o_ref.dtype)


def _mla_attn(q, k, v):
    tq = 256
    nq = SEQ // tq
    return pl.pallas_call(
        functools.partial(_mla_attn_kernel, tq),
        grid=(BATCH, MLA_HEADS, nq),
        in_specs=[pl.BlockSpec((tq, MLA_QK_PAD), lambda b, h, i: (b * nq + i, h)),
                  pl.BlockSpec((SEQ, MLA_QK_PAD), lambda b, h, i: (b, h)),
                  pl.BlockSpec((SEQ, MLA_V), lambda b, h, i: (b, h))],
        out_specs=pl.BlockSpec((tq, MLA_V), lambda b, h, i: (b * nq + i, h)),
        out_shape=jax.ShapeDtypeStruct((TOKENS, MLA_HEADS * MLA_V), BF16),
        compiler_params=_params("arbitrary", "arbitrary", "arbitrary"),
        name="mla_attn",
    )(q, k, v)


def _xa_attn_kernel(q_ref, kv_ref, o_ref):
    for hh in range(XA_HEADS):
        cols = slice(hh * XA_DIM, (hh + 1) * XA_DIM)
        vcols = slice(D_MODEL + hh * XA_DIM, D_MODEL + (hh + 1) * XA_DIM)
        s = lax.dot_general(q_ref[:, cols], kv_ref[:, cols], (((1,), (1,)), ((), ())),
                            preferred_element_type=F32) * (XA_DIM ** -0.5)
        e = jnp.exp(s - jnp.max(s, -1, keepdims=True))
        p = e / jnp.sum(e, -1, keepdims=True)
        o_ref[:, cols] = jnp.dot(p.astype(BF16), kv_ref[:, vcols],
                                 preferred_element_type=F32).astype(o_ref.dtype)


def _xa_attn(q, kv):
    tq = 1024
    nq = SEQ // tq
    return pl.pallas_call(
        _xa_attn_kernel,
        grid=(BATCH, nq),
        in_specs=[pl.BlockSpec((tq, D_MODEL), lambda b, i: (b * nq + i, 0)),
                  pl.BlockSpec((MEM_LEN, 2 * D_MODEL), lambda b, i: (b, 0))],
        out_specs=pl.BlockSpec((tq, D_MODEL), lambda b, i: (b * nq + i, 0)),
        out_shape=jax.ShapeDtypeStruct((TOKENS, D_MODEL), BF16),
        compiler_params=_params("arbitrary", "arbitrary"),
        name="xa_attn",
    )(q, kv)


def _mla_weights(w_in, w_uq, w_ukv):
    pad = LANES - MLA_ROPE
    w_in = jnp.pad(w_in, ((0, 0), (0, pad)))
    w_uq = w_uq.reshape(MLA_Q_RANK, MLA_HEADS, MLA_NOPE + MLA_ROPE)
    w_uq = jnp.pad(w_uq, ((0, 0), (0, 0), (0, pad))).reshape(MLA_Q_RANK, MLA_HEADS * MLA_QK_PAD)
    w_ukv = w_ukv.reshape(MLA_KV_RANK, MLA_HEADS, MLA_NOPE + MLA_V)
    w_uk = w_ukv[:, :, :MLA_NOPE].reshape(MLA_KV_RANK, MLA_HEADS * MLA_NOPE)
    w_uv = w_ukv[:, :, MLA_NOPE:].reshape(MLA_KV_RANK, MLA_HEADS * MLA_V)
    return w_in.astype(BF16), w_uq.astype(BF16), w_uk.astype(BF16), w_uv.astype(BF16)


def kernel(x, mem, positions, ret_w_in, ret_gn_g, ret_w_out, mla_w_in, mla_q_norm_g, mla_w_uq, mla_kv_norm_g, mla_w_ukv, mla_w_out, xa_w_q, xa_w_kv, xa_w_out, ffn_w_up, ffn_conv_w, ffn_conv_b, ffn_w_down, ln_mix_g, ln_mix_b, ln_mem_g, ln_mem_b, ln_ffn_g, ln_ffn_b):
    h = x.reshape(TOKENS, D_MODEL)
    hb = h.astype(BF16)
    mem_b = mem.reshape(BATCH * MEM_LEN, D_MODEL).astype(BF16)
    cos_r, sin_r, c_m, sp_m, sn_m = _rope_tables(positions)

    for layer in range(DEPTH):
        j = layer // N_MIXERS
        if layer % N_MIXERS == 0:
            qk = _ret_qk(hb, ret_w_in, j, cos_r, sin_r)
            vg = _xres_matmul(hb, ret_w_in, j, 2 * RET_QK_WIDTH, 2 * RET_V_WIDTH, tn=512, name="ret_vg")
            mix_in = _ret_core(qk, vg, ret_gn_g[j])
            w_out = ret_w_out[j]
        else:
            w_in, w_uq, w_uk, w_uv = _mla_weights(mla_w_in[j], mla_w_uq[j], mla_w_ukv[j])
            q, k, v = _mla_proj(hb, w_in, mla_q_norm_g[j].reshape(1, MLA_Q_RANK),
                                mla_kv_norm_g[j].reshape(1, MLA_KV_RANK), w_uq, w_uk, w_uv, c_m, sp_m, sn_m)
            mix_in = _mla_attn(q, k, v)
            w_out = mla_w_out[j]
        h, hb = _mm_res_ln(mix_in, w_out.astype(BF16), h, ln_mix_g[layer], ln_mix_b[layer],
                           tm=512, name="mix_out_ln")

        xq = _xres_matmul(hb, xa_w_q, layer, 0, D_MODEL, tn=512, name="xa_q")
        xkv = _xres_matmul(mem_b, xa_w_kv, layer, 0, 2 * D_MODEL, tn=512, name="xa_kv")
        xo = _xa_attn(xq, xkv)
        h, hb = _mm_res_ln(xo, xa_w_out[layer].astype(BF16), h, ln_mem_g[layer], ln_mem_b[layer],
                           tm=512, name="xa_out_ln")

        act = _ffn_up(hb, ffn_w_up, ffn_conv_w, ffn_conv_b, layer)
        h, hb = _mm_res_ln(act, ffn_w_down[layer].astype(BF16), h, ln_ffn_g[layer], ln_ffn_b[layer],
                           tm=256, name="ffn_down_ln")

    return h.reshape(BATCH, SEQ, D_MODEL)
```

```python
import functools

import jax
import jax.numpy as jnp
from jax import lax
from jax.experimental import pallas as pl
from jax.experimental.pallas import tpu as pltpu

D_MODEL = 2048
BATCH = 4
SEQ = 2048
DEPTH = 4
CHUNK = 64
MEM_LEN = 256
N_MIXERS = 2

RET_HEADS = 8
RET_QK_DIM = D_MODEL // RET_HEADS
RET_V_DIM = 2 * D_MODEL // RET_HEADS
RET_QK_WIDTH = RET_HEADS * RET_QK_DIM
RET_V_WIDTH = RET_HEADS * RET_V_DIM

MLA_HEADS = 16
MLA_Q_RANK = 512
MLA_KV_RANK = 512
MLA_NOPE = 128
MLA_ROPE = 64
MLA_V = 128

XA_HEADS = 4
XA_DIM = D_MODEL // XA_HEADS

D_FF = 5632
CONV_W = 3

ROPE_BASE = 10000.0
LN_EPS = 1e-5
RMS_EPS = 1e-6
NEG_INF = -1e30
DEEPNORM_ALPHA = (2 * DEPTH) ** 0.25
LOG2_E = 1.4426950408889634

TOKENS = BATCH * SEQ
LANES = 128
BF16_SUBLANES = 16
MLA_QK_PAD = 256
RET_BLOCK = 256
VMEM_LIMIT = 56 * 1024 * 1024

X_ROWS = 2 * SEQ
ROW_CHUNK = 1024
FFN_ROW_CHUNK = 512
LN_SUB = 256

F32 = jnp.float32
BF16 = jnp.bfloat16


def _params(*semantics):
    return pltpu.CompilerParams(dimension_semantics=semantics, vmem_limit_bytes=VMEM_LIMIT)


def _single(shape, index_map):
    return pl.BlockSpec(shape, index_map, pipeline_mode=pl.Buffered(1))


def _rope_tables_kernel(pos_ref, invf_ret_ref, invf_mla_ref, cos_r, sin_r, c_m, sp_m, sn_m):
    pos = pos_ref[...].astype(F32)
    ang = pos * invf_ret_ref[...]
    cos_r[...] = jnp.cos(ang)
    sin_r[...] = jnp.sin(ang)
    angm = pos * invf_mla_ref[...]
    lane = lax.broadcasted_iota(jnp.int32, angm.shape, 1)
    half = MLA_ROPE // 2
    c = jnp.cos(angm)
    s = jnp.sin(angm)
    c_m[...] = jnp.where(lane < MLA_ROPE, c, 0.0)
    sp_m[...] = jnp.where((lane >= half) & (lane < MLA_ROPE), s, 0.0)
    sn_m[...] = jnp.where(lane < half, -s, 0.0)


def _rope_tables(positions):
    tm = 1024
    pos = positions.reshape(TOKENS, 1)
    invf_ret = ROPE_BASE ** (-jnp.arange(0, RET_QK_DIM, 2, dtype=F32) / RET_QK_DIM)
    invf_mla = ROPE_BASE ** (-jnp.arange(0, MLA_ROPE, 2, dtype=F32) / MLA_ROPE)
    invf_mla = jnp.concatenate([invf_mla, invf_mla, jnp.zeros((LANES - MLA_ROPE,), F32)])
    row = pl.BlockSpec((tm, LANES), lambda i: (i, 0))
    const = pl.BlockSpec((1, LANES), lambda i: (0, 0))
    return pl.pallas_call(
        _rope_tables_kernel,
        grid=(TOKENS // tm,),
        in_specs=[pl.BlockSpec((tm, 1), lambda i: (i, 0)), const, const],
        out_specs=[row] * 5,
        out_shape=[jax.ShapeDtypeStruct((TOKENS, LANES), F32)] * 5,
        compiler_params=_params("arbitrary"),
        name="rope_tables",
    )(pos, invf_ret.reshape(1, LANES), invf_mla.reshape(1, LANES))


def _rope_mla(x, c, sp, sn):
    half = MLA_ROPE // 2
    return x * c + pltpu.roll(x, half, 1) * sp + pltpu.roll(x, LANES - half, 1) * sn


def _row_chunks(rows):
    chunk = min(ROW_CHUNK, rows)
    return [(r, chunk) for r in range(0, rows, chunk)]


def _xres_matmul_kernel(x_ref, w_ref, o_ref):
    w = w_ref[...].astype(BF16)
    for r, n in _row_chunks(x_ref.shape[0]):
        o_ref[r:r + n, :] = jnp.dot(x_ref[r:r + n, :], w, preferred_element_type=F32).astype(o_ref.dtype)


def _w_cols(w_stack, layer, col0, tn):
    k = w_stack.shape[1]
    first = col0 // tn
    return pl.BlockSpec((None, k, tn), lambda i, j: (layer, 0, first + j))


def _xres_matmul(x, w_stack, layer, col0, n, tn, name):
    m, k = x.shape
    rows = min(X_ROWS, m)
    return pl.pallas_call(
        _xres_matmul_kernel,
        grid=(m // rows, n // tn),
        in_specs=[_single((rows, k), lambda i, j: (i, 0)), _w_cols(w_stack, layer, col0, tn)],
        out_specs=pl.BlockSpec((rows, tn), lambda i, j: (i, j)),
        out_shape=jax.ShapeDtypeStruct((m, n), BF16),
        compiler_params=_params("arbitrary", "arbitrary"),
        name=name,
    )(x, w_stack)


def _ret_qk_kernel(tn, x_ref, w_ref, cos_ref, sin_ref, o_ref):
    j = pl.program_id(1)
    w = w_ref[...].astype(BF16)
    scale = jnp.where(j >= RET_QK_WIDTH // tn, RET_QK_DIM ** -0.5, 1.0).astype(F32)
    half = RET_QK_DIM // 2
    for r, n in _row_chunks(x_ref.shape[0]):
        acc = jnp.dot(x_ref[r:r + n, :], w, preferred_element_type=F32)
        cos = cos_ref[r:r + n, :]
        sin = sin_ref[r:r + n, :]
        for lo in range(0, tn, RET_QK_DIM):
            x1 = acc[:, lo:lo + half]
            x2 = acc[:, lo + half:lo + RET_QK_DIM]
            o_ref[r:r + n, lo:lo + half] = ((x1 * cos - x2 * sin) * scale).astype(o_ref.dtype)
            o_ref[r:r + n, lo + half:lo + RET_QK_DIM] = ((x2 * cos + x1 * sin) * scale).astype(o_ref.dtype)


def _ret_qk(xb, w_stack, layer, cos, sin):
    tn = 512
    n = 2 * RET_QK_WIDTH
    rows = X_ROWS
    return pl.pallas_call(
        functools.partial(_ret_qk_kernel, tn),
        grid=(TOKENS // rows, n // tn),
        in_specs=[_single((rows, D_MODEL), lambda i, j: (i, 0)),
                  _w_cols(w_stack, layer, 0, tn),
                  _single((rows, LANES), lambda i, j: (i, 0)),
                  _single((rows, LANES), lambda i, j: (i, 0))],
        out_specs=pl.BlockSpec((rows, tn), lambda i, j: (i, j)),
        out_shape=jax.ShapeDtypeStruct((TOKENS, n), BF16),
        compiler_params=_params("arbitrary", "arbitrary"),
        name="ret_qk",
    )(xb, w_stack, cos, sin)


def _ffn_up_kernel(x_ref, wg_ref, wv_ref, cwg_ref, cwv_ref, cbg_ref, cbv_ref, o_ref, hg_ref, hv_ref):
    halo = BF16_SUBLANES
    wg = wg_ref[...].astype(BF16)
    wv = wv_ref[...].astype(BF16)
    for r in range(0, x_ref.shape[0], FFN_ROW_CHUNK):
        n = FFN_ROW_CHUNK
        seq_start = r % SEQ == 0

        def conv(w, cw_ref, cb_ref, h_ref):
            if seq_start:
                h_ref[0:halo, :] = jnp.zeros((halo, h_ref.shape[1]), F32)
                h_ref[halo:, :] = jnp.dot(x_ref[r:r + n, :], w, preferred_element_type=F32)
            else:
                h_ref[...] = jnp.dot(x_ref[r - halo:r + n, :], w, preferred_element_type=F32)
            cw = cw_ref[...]
            out = (cw[2:3] * h_ref[halo:halo + n, :] + cw[1:2] * h_ref[halo - 1:halo - 1 + n, :]
                   + cw[0:1] * h_ref[halo - 2:halo - 2 + n, :])
            return out + cb_ref[...]

        gate = conv(wg, cwg_ref, cbg_ref, hg_ref)
        val = conv(wv, cwv_ref, cbv_ref, hv_ref)
        o_ref[r:r + n, :] = (gate * jax.nn.sigmoid(gate) * val).astype(o_ref.dtype)


def _ffn_up(xb, w_up, conv_w, conv_b, layer):
    tn = 256
    nj = D_FF // tn
    rows = X_ROWS
    conv_b = conv_b.reshape(DEPTH, 1, 2 * D_FF)
    taps = lambda depth, first: pl.BlockSpec((None, depth, tn), lambda i, j: (layer, 0, first + j))
    return pl.pallas_call(
        _ffn_up_kernel,
        grid=(TOKENS // rows, nj),
        in_specs=[_single((rows, D_MODEL), lambda i, j: (i, 0)),
                  _w_cols(w_up, layer, 0, tn), _w_cols(w_up, layer, D_FF, tn),
                  taps(CONV_W, 0), taps(CONV_W, nj), taps(1, 0), taps(1, nj)],
        out_specs=pl.BlockSpec((rows, tn), lambda i, j: (i, j)),
        out_shape=jax.ShapeDtypeStruct((TOKENS, D_FF), BF16),
        scratch_shapes=[pltpu.VMEM((BF16_SUBLANES + FFN_ROW_CHUNK, tn), F32)] * 2,
        compiler_params=_params("arbitrary", "arbitrary"),
        name="ffn_up",
    )(xb, w_up, w_up, conv_w, conv_w, conv_b, conv_b)


def _mm_res_ln_kernel(a_ref, w_ref, h_ref, g_ref, b_ref, of_ref, ob_ref):
    g = g_ref[...]
    b = b_ref[...]
    for r in range(0, a_ref.shape[0], LN_SUB):
        rows = slice(r, r + LN_SUB)
        y = DEEPNORM_ALPHA * h_ref[rows, :] + jnp.dot(a_ref[rows, :], w_ref[...], preferred_element_type=F32)
        mu = jnp.mean(y, -1, keepdims=True)
        d = y - mu
        var = jnp.mean(d * d, -1, keepdims=True)
        out = d * lax.rsqrt(var + LN_EPS) * g + b
        of_ref[rows, :] = out
        ob_ref[rows, :] = out.astype(BF16)


def _mm_res_ln(a, w, h, g, b, tm, name):
    m, k = a.shape
    n = w.shape[1]
    row = lambda width: pl.BlockSpec((tm, width), lambda i: (i, 0))
    vec = pl.BlockSpec((1, n), lambda i: (0, 0))
    return pl.pallas_call(
        _mm_res_ln_kernel,
        grid=(m // tm,),
        in_specs=[row(k), _single((k, n), lambda i: (0, 0)), row(n), vec, vec],
        out_specs=[row(n), row(n)],
        out_shape=[jax.ShapeDtypeStruct((m, n), F32), jax.ShapeDtypeStruct((m, n), BF16)],
        compiler_params=_params("arbitrary"),
        name=name,
    )(a, w, h, g.reshape(1, n), b.reshape(1, n))


def _ret_core_kernel(lg_ref, q_ref, k_ref, v_ref, g_ref, gn_ref, y_ref, state_ref, decay_ref):
    L = RET_BLOCK
    head = pl.program_id(1)
    step = pl.program_id(2)
    lg = lg_ref[head]

    @pl.when(step == 0)
    def _():
        state_ref[...] = jnp.zeros_like(state_ref)
        n = lax.broadcasted_iota(jnp.int32, (L, L), 0)
        m = lax.broadcasted_iota(jnp.int32, (L, L), 1)
        dist = jnp.abs(n - m).astype(F32)
        decay_ref[...] = jnp.where((m // CHUNK) <= (n // CHUNK), jnp.exp(lg * dist), 0.0)

    idx = lax.broadcasted_iota(jnp.int32, (L, 1), 0).astype(F32)
    q_decay = jnp.exp(idx * lg)
    k_decay = jnp.exp((L - idx) * lg)
    block_decay = jnp.exp(jnp.full((1, RET_V_DIM), L, F32) * lg)
    decay = decay_ref[...]
    gn = gn_ref[...]
    state = state_ref[...]
    for r in range(0, q_ref.shape[0], L):
        rows = slice(r, r + L)
        q = q_ref[rows, :]
        k = k_ref[rows, :]
        v = v_ref[rows, :]
        s = lax.dot_general(q, k, (((1,), (1,)), ((), ())), preferred_element_type=F32)
        o = jnp.dot((s * decay).astype(BF16), v, preferred_element_type=F32)
        o = o + jnp.dot(q, state.astype(BF16), preferred_element_type=F32) * q_decay
        k_dec = (k.astype(F32) * k_decay).astype(BF16)
        upd = lax.dot_general(k_dec, v, (((0,), (0,)), ((), ())), preferred_element_type=F32)
        state = state * block_decay + upd

        mu = jnp.mean(o, -1, keepdims=True)
        d = o - mu
        var = jnp.mean(d * d, -1, keepdims=True)
        normed = d * lax.rsqrt(var + LN_EPS) * gn
        gate = g_ref[rows, :].astype(F32)
        y_ref[rows, :] = (gate * jax.nn.sigmoid(gate) * normed).astype(y_ref.dtype)
    state_ref[...] = state


def _ret_core(qk, vg, gn_g):
    ts = 1024
    ns = SEQ // ts
    log_gamma = jnp.log(1.0 - 2.0 ** (-5.0 - jnp.arange(RET_HEADS, dtype=F32)))
    return pl.pallas_call(
        _ret_core_kernel,
        grid=(BATCH, RET_HEADS, ns),
        in_specs=[pl.BlockSpec(memory_space=pltpu.SMEM),
                  pl.BlockSpec((ts, RET_QK_DIM), lambda b, h, c: (b * ns + c, h)),
                  pl.BlockSpec((ts, RET_QK_DIM), lambda b, h, c: (b * ns + c, RET_HEADS + h)),
                  pl.BlockSpec((ts, RET_V_DIM), lambda b, h, c: (b * ns + c, h)),
                  pl.BlockSpec((ts, RET_V_DIM), lambda b, h, c: (b * ns + c, RET_HEADS + h)),
                  pl.BlockSpec((1, RET_V_DIM), lambda b, h, c: (0, h))],
        out_specs=pl.BlockSpec((ts, RET_V_DIM), lambda b, h, c: (b * ns + c, h)),
        out_shape=jax.ShapeDtypeStruct((TOKENS, RET_V_WIDTH), BF16),
        scratch_shapes=[pltpu.VMEM((RET_QK_DIM, RET_V_DIM), F32),
                        pltpu.VMEM((RET_BLOCK, RET_BLOCK), F32)],
        compiler_params=_params("arbitrary", "arbitrary", "arbitrary"),
        name="ret_core",
    )(log_gamma, qk, qk, vg, vg, gn_g.reshape(1, RET_V_WIDTH))


def _rms(x, g):
    return x * lax.rsqrt(jnp.mean(x * x, -1, keepdims=True) + RMS_EPS) * g


def _mla_proj_kernel(x_ref, w_in_ref, qg_ref, kvg_ref, w_uq_ref, w_uk_ref, w_uv_ref,
                     c_ref, sp_ref, sn_ref, q_ref, k_ref, v_ref):
    c = c_ref[...]
    sp = sp_ref[...]
    sn = sn_ref[...]
    proj = jnp.dot(x_ref[...], w_in_ref[...], preferred_element_type=F32)
    c_q = _rms(proj[:, :MLA_Q_RANK], qg_ref[...]).astype(BF16)
    c_kv = _rms(proj[:, MLA_Q_RANK:MLA_Q_RANK + MLA_KV_RANK], kvg_ref[...]).astype(BF16)
    k_rope = _rope_mla(proj[:, MLA_Q_RANK + MLA_KV_RANK:], c, sp, sn).astype(BF16)

    q = jnp.dot(c_q, w_uq_ref[...], preferred_element_type=F32)
    k_nope = jnp.dot(c_kv, w_uk_ref[...], preferred_element_type=F32)
    for hh in range(MLA_HEADS):
        lo = hh * MLA_QK_PAD
        q_ref[:, lo:lo + MLA_NOPE] = q[:, lo:lo + MLA_NOPE].astype(BF16)
        q_ref[:, lo + MLA_NOPE:lo + MLA_QK_PAD] = _rope_mla(
            q[:, lo + MLA_NOPE:lo + MLA_QK_PAD], c, sp, sn).astype(BF16)
        k_ref[:, lo:lo + MLA_NOPE] = k_nope[:, hh * MLA_NOPE:(hh + 1) * MLA_NOPE].astype(BF16)
        k_ref[:, lo + MLA_NOPE:lo + MLA_QK_PAD] = k_rope
    v_ref[...] = jnp.dot(c_kv, w_uv_ref[...], preferred_element_type=F32).astype(BF16)


def _mla_proj(xb, w_in, qg, kvg, w_uq, w_uk, w_uv, c, sp, sn):
    tm = 256
    qk_w = MLA_HEADS * MLA_QK_PAD
    v_w = MLA_HEADS * MLA_V
    row = lambda w: pl.BlockSpec((tm, w), lambda i: (i, 0))
    res = lambda a: _single(a.shape, lambda i: (0, 0))
    return pl.pallas_call(
        _mla_proj_kernel,
        grid=(TOKENS // tm,),
        in_specs=[row(D_MODEL), res(w_in), res(qg), res(kvg), res(w_uq), res(w_uk), res(w_uv),
                  row(LANES), row(LANES), row(LANES)],
        out_specs=[row(qk_w), row(qk_w), row(v_w)],
        out_shape=[jax.ShapeDtypeStruct((TOKENS, qk_w), BF16),
                   jax.ShapeDtypeStruct((TOKENS, qk_w), BF16),
                   jax.ShapeDtypeStruct((TOKENS, v_w), BF16)],
        compiler_params=_params("arbitrary"),
        name="mla_proj",
    )(xb, w_in, qg, kvg, w_uq, w_uk, w_uv, c, sp, sn)


def _mla_attn_kernel(tq, q_ref, k_ref, v_ref, o_ref, s_ref):
    c_exp = (MLA_NOPE + MLA_ROPE) ** -0.5 * LOG2_E
    dims = (((1,), (1,)), ((), ()))
    row = lax.broadcasted_iota(jnp.int32, (tq, tq), 0)
    col = lax.broadcasted_iota(jnp.int32, (tq, tq), 1)
    visible = (row // CHUNK) >= (col // CHUNK)
    for c in range(SEQ // tq):
        lo = c * tq
        n = lo + tq
        buf = s_ref.at[c % 2]
        buf[:, 0:n] = lax.dot_general(q_ref[lo:n, :], k_ref[0:n, :], dims, preferred_element_type=F32) * c_exp
        buf[:, lo:n] = jnp.where(visible, buf[:, lo:n], NEG_INF)
        s = buf[:, 0:n]
        p = jnp.exp2(s - jnp.max(s, -1, keepdims=True))
        l = jnp.sum(p, -1, keepdims=True)
        acc = jnp.dot(p.astype(BF16), v_ref[0:n, :], preferred_element_type=F32)
        o_ref[lo:n, :] = (acc * (1.0 / l)).astype(o_ref.dtype)


def _mla_attn(q, k, v):
    tq = 256
    return pl.pallas_call(
        functools.partial(_mla_attn_kernel, tq),
        grid=(BATCH, MLA_HEADS),
        in_specs=[pl.BlockSpec((SEQ, MLA_QK_PAD), lambda b, h: (b, h)),
                  pl.BlockSpec((SEQ, MLA_QK_PAD), lambda b, h: (b, h)),
                  pl.BlockSpec((SEQ, MLA_V), lambda b, h: (b, h))],
        out_specs=pl.BlockSpec((SEQ, MLA_V), lambda b, h: (b, h)),
        out_shape=jax.ShapeDtypeStruct((TOKENS, MLA_HEADS * MLA_V), BF16),
        scratch_shapes=[pltpu.VMEM((2, tq, SEQ), F32)],
        compiler_params=_params("arbitrary", "arbitrary"),
        name="mla_attn",
    )(q, k, v)


def _xa_attn_kernel(q_ref, kv_ref, o_ref):
    for hh in range(XA_HEADS):
        cols = slice(hh * XA_DIM, (hh + 1) * XA_DIM)
        vcols = slice(D_MODEL + hh * XA_DIM, D_MODEL + (hh + 1) * XA_DIM)
        s = lax.dot_general(q_ref[:, cols], kv_ref[:, cols], (((1,), (1,)), ((), ())),
                            preferred_element_type=F32) * (XA_DIM ** -0.5)
        e = jnp.exp(s - jnp.max(s, -1, keepdims=True))
        p = e / jnp.sum(e, -1, keepdims=True)
        o_ref[:, cols] = jnp.dot(p.astype(BF16), kv_ref[:, vcols],
                                 preferred_element_type=F32).astype(o_ref.dtype)


def _xa_attn(q, kv):
    tq = 1024
    nq = SEQ // tq
    return pl.pallas_call(
        _xa_attn_kernel,
        grid=(BATCH, nq),
        in_specs=[pl.BlockSpec((tq, D_MODEL), lambda b, i: (b * nq + i, 0)),
                  pl.BlockSpec((MEM_LEN, 2 * D_MODEL), lambda b, i: (b, 0))],
        out_specs=pl.BlockSpec((tq, D_MODEL), lambda b, i: (b * nq + i, 0)),
        out_shape=jax.ShapeDtypeStruct((TOKENS, D_MODEL), BF16),
        compiler_params=_params("arbitrary", "arbitrary"),
        name="xa_attn",
    )(q, kv)


def _mla_weights(w_in, w_uq, w_ukv):
    pad = LANES - MLA_ROPE
    w_in = jnp.pad(w_in, ((0, 0), (0, pad)))
    w_uq = w_uq.reshape(MLA_Q_RANK, MLA_HEADS, MLA_NOPE + MLA_ROPE)
    w_uq = jnp.pad(w_uq, ((0, 0), (0, 0), (0, pad))).reshape(MLA_Q_RANK, MLA_HEADS * MLA_QK_PAD)
    w_ukv = w_ukv.reshape(MLA_KV_RANK, MLA_HEADS, MLA_NOPE + MLA_V)
    w_uk = w_ukv[:, :, :MLA_NOPE].reshape(MLA_KV_RANK, MLA_HEADS * MLA_NOPE)
    w_uv = w_ukv[:, :, MLA_NOPE:].reshape(MLA_KV_RANK, MLA_HEADS * MLA_V)
    return w_in.astype(BF16), w_uq.astype(BF16), w_uk.astype(BF16), w_uv.astype(BF16)


def kernel(x, mem, positions, ret_w_in, ret_gn_g, ret_w_out, mla_w_in, mla_q_norm_g, mla_w_uq, mla_kv_norm_g, mla_w_ukv, mla_w_out, xa_w_q, xa_w_kv, xa_w_out, ffn_w_up, ffn_conv_w, ffn_conv_b, ffn_w_down, ln_mix_g, ln_mix_b, ln_mem_g, ln_mem_b, ln_ffn_g, ln_ffn_b):
    h = x.reshape(TOKENS, D_MODEL)
    hb = h.astype(BF16)
    mem_b = mem.reshape(BATCH * MEM_LEN, D_MODEL).astype(BF16)
    cos_r, sin_r, c_m, sp_m, sn_m = _rope_tables(positions)

    for layer in range(DEPTH):
        j = layer // N_MIXERS
        if layer % N_MIXERS == 0:
            qk = _ret_qk(hb, ret_w_in, j, cos_r, sin_r)
            vg = _xres_matmul(hb, ret_w_in, j, 2 * RET_QK_WIDTH, 2 * RET_V_WIDTH, tn=512, name="ret_vg")
            mix_in = _ret_core(qk, vg, ret_gn_g[j])
            w_out = ret_w_out[j]
        else:
            w_in, w_uq, w_uk, w_uv = _mla_weights(mla_w_in[j], mla_w_uq[j], mla_w_ukv[j])
            q, k, v = _mla_proj(hb, w_in, mla_q_norm_g[j].reshape(1, MLA_Q_RANK),
                                mla_kv_norm_g[j].reshape(1, MLA_KV_RANK), w_uq, w_uk, w_uv, c_m, sp_m, sn_m)
            mix_in = _mla_attn(q, k, v)
            w_out = mla_w_out[j]
        h, hb = _mm_res_ln(mix_in, w_out.astype(BF16), h, ln_mix_g[layer], ln_mix_b[layer],
                           tm=512, name="mix_out_ln")

        xq = _xres_matmul(hb, xa_w_q, layer, 0, D_MODEL, tn=512, name="xa_q")
        xkv = _xres_matmul(mem_b, xa_w_kv, layer, 0, 2 * D_MODEL, tn=512, name="xa_kv")
        xo = _xa_attn(xq, xkv)
        h, hb = _mm_res_ln(xo, xa_w_out[layer].astype(BF16), h, ln_mem_g[layer], ln_mem_b[layer],
                           tm=512, name="xa_out_ln")

        act = _ffn_up(hb, ffn_w_up, ffn_conv_w, ffn_conv_b, layer)
        h, hb = _mm_res_ln(act, ffn_w_down[layer].astype(BF16), h, ln_ffn_g[layer], ln_ffn_b[layer],
                           tm=256, name="ffn_down_ln")

    return h.reshape(BATCH, SEQ, D_MODEL)
```

```python
import functools

import jax
import jax.numpy as jnp
from jax import lax
from jax.experimental import pallas as pl
from jax.experimental.pallas import tpu as pltpu

D_MODEL = 2048
BATCH = 4
SEQ = 2048
DEPTH = 4
CHUNK = 64
MEM_LEN = 256
N_MIXERS = 2

RET_HEADS = 8
RET_QK_DIM = D_MODEL // RET_HEADS
RET_V_DIM = 2 * D_MODEL // RET_HEADS
RET_QK_WIDTH = RET_HEADS * RET_QK_DIM
RET_V_WIDTH = RET_HEADS * RET_V_DIM

MLA_HEADS = 16
MLA_Q_RANK = 512
MLA_KV_RANK = 512
MLA_NOPE = 128
MLA_ROPE = 64
MLA_V = 128

XA_HEADS = 4
XA_DIM = D_MODEL // XA_HEADS

D_FF = 5632
CONV_W = 3

ROPE_BASE = 10000.0
LN_EPS = 1e-5
RMS_EPS = 1e-6
NEG_INF = -1e30
DEEPNORM_ALPHA = (2 * DEPTH) ** 0.25
LOG2_E = 1.4426950408889634

TOKENS = BATCH * SEQ
LANES = 128
BF16_SUBLANES = 16
MLA_QK_PAD = 256
RET_BLOCK = 256
VMEM_LIMIT = 56 * 1024 * 1024

X_ROWS = 2 * SEQ
ROW_CHUNK = 1024
FFN_ROW_CHUNK = 512
LN_SUB = 256

F32 = jnp.float32
BF16 = jnp.bfloat16


def _params(*semantics):
    return pltpu.CompilerParams(dimension_semantics=semantics, vmem_limit_bytes=VMEM_LIMIT)


def _single(shape, index_map):
    return pl.BlockSpec(shape, index_map, pipeline_mode=pl.Buffered(1))


def _rope_tables_kernel(pos_ref, invf_ret_ref, invf_mla_ref, cos_r, sin_r, c_m, sp_m, sn_m):
    pos = pos_ref[...].astype(F32)
    ang = pos * invf_ret_ref[...]
    cos_r[...] = jnp.cos(ang)
    sin_r[...] = jnp.sin(ang)
    angm = pos * invf_mla_ref[...]
    lane = lax.broadcasted_iota(jnp.int32, angm.shape, 1)
    half = MLA_ROPE // 2
    c = jnp.cos(angm)
    s = jnp.sin(angm)
    c_m[...] = jnp.where(lane < MLA_ROPE, c, 0.0)
    sp_m[...] = jnp.where((lane >= half) & (lane < MLA_ROPE), s, 0.0)
    sn_m[...] = jnp.where(lane < half, -s, 0.0)


def _rope_tables(positions):
    tm = 1024
    pos = positions.reshape(TOKENS, 1)
    invf_ret = ROPE_BASE ** (-jnp.arange(0, RET_QK_DIM, 2, dtype=F32) / RET_QK_DIM)
    invf_mla = ROPE_BASE ** (-jnp.arange(0, MLA_ROPE, 2, dtype=F32) / MLA_ROPE)
    invf_mla = jnp.concatenate([invf_mla, invf_mla, jnp.zeros((LANES - MLA_ROPE,), F32)])
    row = pl.BlockSpec((tm, LANES), lambda i: (i, 0))
    const = pl.BlockSpec((1, LANES), lambda i: (0, 0))
    return pl.pallas_call(
        _rope_tables_kernel,
        grid=(TOKENS // tm,),
        in_specs=[pl.BlockSpec((tm, 1), lambda i: (i, 0)), const, const],
        out_specs=[row] * 5,
        out_shape=[jax.ShapeDtypeStruct((TOKENS, LANES), F32)] * 5,
        compiler_params=_params("arbitrary"),
        name="rope_tables",
    )(pos, invf_ret.reshape(1, LANES), invf_mla.reshape(1, LANES))


def _rope_mla(x, c, sp, sn):
    half = MLA_ROPE // 2
    return x * c + pltpu.roll(x, half, 1) * sp + pltpu.roll(x, LANES - half, 1) * sn


def _row_chunks(rows):
    chunk = min(ROW_CHUNK, rows)
    return [(r, chunk) for r in range(0, rows, chunk)]


def _cast_slab(src_ref, dst_ref):
    dst_ref[...] = src_ref[...].astype(BF16)


def _slab_specs(w_stack, layer, grid):
    k, n = w_stack.shape[1:]
    steps = 1
    for g in grid:
        steps *= g
    rows = k // steps
    assert rows * steps == k and rows % BF16_SUBLANES == 0
    if len(grid) == 1:
        flat = lambda i: i
    else:
        flat = lambda i, j: i * grid[1] + j
    src = pl.BlockSpec((None, rows, n), lambda *ids: (layer, flat(*ids), 0))
    dst = pl.BlockSpec((rows, n), lambda *ids: (flat(*ids), 0))
    return src, dst, jax.ShapeDtypeStruct((k, n), BF16)


def _xres_matmul_kernel(x_ref, w_ref, o_ref):
    w = w_ref[...].astype(BF16)
    for r, n in _row_chunks(x_ref.shape[0]):
        o_ref[r:r + n, :] = jnp.dot(x_ref[r:r + n, :], w, preferred_element_type=F32).astype(o_ref.dtype)


def _xres_matmul_cast_kernel(x_ref, w_ref, slab_ref, o_ref, slab_out_ref):
    _cast_slab(slab_ref, slab_out_ref)
    _xres_matmul_kernel(x_ref, w_ref, o_ref)


def _w_cols(w_stack, layer, col0, tn):
    k = w_stack.shape[1]
    first = col0 // tn
    return pl.BlockSpec((None, k, tn), lambda i, j: (layer, 0, first + j))


def _xres_matmul(x, w_stack, layer, col0, n, tn, name, cast=None):
    m, k = x.shape
    rows = min(X_ROWS, m)
    grid = (m // rows, n // tn)
    in_specs = [_single((rows, k), lambda i, j: (i, 0)), _w_cols(w_stack, layer, col0, tn)]
    out_specs = [pl.BlockSpec((rows, tn), lambda i, j: (i, j))]
    out_shape = [jax.ShapeDtypeStruct((m, n), BF16)]
    args = [x, w_stack]
    body = _xres_matmul_kernel
    if cast is not None:
        src, dst, shape = _slab_specs(cast[0], cast[1], grid)
        in_specs.append(src)
        out_specs.append(dst)
        out_shape.append(shape)
        args.append(cast[0])
        body = _xres_matmul_cast_kernel
    out = pl.pallas_call(
        body,
        grid=grid,
        in_specs=in_specs,
        out_specs=out_specs,
        out_shape=out_shape,
        compiler_params=_params("arbitrary", "arbitrary"),
        name=name,
    )(*args)
    return out if cast is not None else out[0]


def _ret_qk_kernel(tn, x_ref, w_ref, cos_ref, sin_ref, o_ref):
    j = pl.program_id(1)
    w = w_ref[...].astype(BF16)
    scale = jnp.where(j >= RET_QK_WIDTH // tn, RET_QK_DIM ** -0.5, 1.0).astype(F32)
    half = RET_QK_DIM // 2
    for r, n in _row_chunks(x_ref.shape[0]):
        acc = jnp.dot(x_ref[r:r + n, :], w, preferred_element_type=F32)
        cos = cos_ref[r:r + n, :]
        sin = sin_ref[r:r + n, :]
        for lo in range(0, tn, RET_QK_DIM):
            x1 = acc[:, lo:lo + half]
            x2 = acc[:, lo + half:lo + RET_QK_DIM]
            o_ref[r:r + n, lo:lo + half] = ((x1 * cos - x2 * sin) * scale).astype(o_ref.dtype)
            o_ref[r:r + n, lo + half:lo + RET_QK_DIM] = ((x2 * cos + x1 * sin) * scale).astype(o_ref.dtype)


def _ret_qk(xb, w_stack, layer, cos, sin):
    tn = 512
    n = 2 * RET_QK_WIDTH
    rows = X_ROWS
    return pl.pallas_call(
        functools.partial(_ret_qk_kernel, tn),
        grid=(TOKENS // rows, n // tn),
        in_specs=[_single((rows, D_MODEL), lambda i, j: (i, 0)),
                  _w_cols(w_stack, layer, 0, tn),
                  _single((rows, LANES), lambda i, j: (i, 0)),
                  _single((rows, LANES), lambda i, j: (i, 0))],
        out_specs=pl.BlockSpec((rows, tn), lambda i, j: (i, j)),
        out_shape=jax.ShapeDtypeStruct((TOKENS, n), BF16),
        compiler_params=_params("arbitrary", "arbitrary"),
        name="ret_qk",
    )(xb, w_stack, cos, sin)


def _ffn_up_kernel(x_ref, wg_ref, wv_ref, cwg_ref, cwv_ref, cbg_ref, cbv_ref, slab_ref,
                   o_ref, slab_out_ref, hg_ref, hv_ref):
    _cast_slab(slab_ref, slab_out_ref)
    halo = BF16_SUBLANES
    wg = wg_ref[...].astype(BF16)
    wv = wv_ref[...].astype(BF16)
    for r in range(0, x_ref.shape[0], FFN_ROW_CHUNK):
        n = FFN_ROW_CHUNK
        seq_start = r % SEQ == 0

        def conv(w, cw_ref, cb_ref, h_ref):
            if seq_start:
                h_ref[0:halo, :] = jnp.zeros((halo, h_ref.shape[1]), F32)
                h_ref[halo:, :] = jnp.dot(x_ref[r:r + n, :], w, preferred_element_type=F32)
            else:
                h_ref[...] = jnp.dot(x_ref[r - halo:r + n, :], w, preferred_element_type=F32)
            cw = cw_ref[...]
            out = (cw[2:3] * h_ref[halo:halo + n, :] + cw[1:2] * h_ref[halo - 1:halo - 1 + n, :]
                   + cw[0:1] * h_ref[halo - 2:halo - 2 + n, :])
            return out + cb_ref[...]

        gate = conv(wg, cwg_ref, cbg_ref, hg_ref)
        val = conv(wv, cwv_ref, cbv_ref, hv_ref)
        o_ref[r:r + n, :] = (gate * jax.nn.sigmoid(gate) * val).astype(o_ref.dtype)


def _ffn_up(xb, w_up, conv_w, conv_b, w_down, layer):
    tn = 256
    nj = D_FF // tn
    rows = X_ROWS
    grid = (TOKENS // rows, nj)
    conv_b = conv_b.reshape(DEPTH, 1, 2 * D_FF)
    taps = lambda depth, first: pl.BlockSpec((None, depth, tn), lambda i, j: (layer, 0, first + j))
    slab_src, slab_dst, slab_shape = _slab_specs(w_down, layer, grid)
    return pl.pallas_call(
        _ffn_up_kernel,
        grid=grid,
        in_specs=[_single((rows, D_MODEL), lambda i, j: (i, 0)),
                  _w_cols(w_up, layer, 0, tn), _w_cols(w_up, layer, D_FF, tn),
                  taps(CONV_W, 0), taps(CONV_W, nj), taps(1, 0), taps(1, nj), slab_src],
        out_specs=[pl.BlockSpec((rows, tn), lambda i, j: (i, j)), slab_dst],
        out_shape=[jax.ShapeDtypeStruct((TOKENS, D_FF), BF16), slab_shape],
        scratch_shapes=[pltpu.VMEM((BF16_SUBLANES + FFN_ROW_CHUNK, tn), F32)] * 2,
        compiler_params=_params("arbitrary", "arbitrary"),
        name="ffn_up",
    )(xb, w_up, w_up, conv_w, conv_w, conv_b, conv_b, w_down)


def _mm_res_ln_kernel(a_ref, w_ref, h_ref, g_ref, b_ref, of_ref, ob_ref):
    g = g_ref[...]
    b = b_ref[...]
    for r in range(0, a_ref.shape[0], LN_SUB):
        rows = slice(r, r + LN_SUB)
        y = DEEPNORM_ALPHA * h_ref[rows, :] + jnp.dot(a_ref[rows, :], w_ref[...], preferred_element_type=F32)
        mu = jnp.mean(y, -1, keepdims=True)
        d = y - mu
        var = jnp.mean(d * d, -1, keepdims=True)
        out = d * lax.rsqrt(var + LN_EPS) * g + b
        of_ref[rows, :] = out
        ob_ref[rows, :] = out.astype(BF16)


def _mm_res_ln(a, w, h, g, b, tm, name):
    m, k = a.shape
    n = w.shape[1]
    row = lambda width: pl.BlockSpec((tm, width), lambda i: (i, 0))
    vec = pl.BlockSpec((1, n), lambda i: (0, 0))
    return pl.pallas_call(
        _mm_res_ln_kernel,
        grid=(m // tm,),
        in_specs=[row(k), _single((k, n), lambda i: (0, 0)), row(n), vec, vec],
        out_specs=[row(n), row(n)],
        out_shape=[jax.ShapeDtypeStruct((m, n), F32), jax.ShapeDtypeStruct((m, n), BF16)],
        compiler_params=_params("arbitrary"),
        name=name,
    )(a, w, h, g.reshape(1, n), b.reshape(1, n))


def _ret_core_kernel(lg_ref, q_ref, k_ref, v_ref, g_ref, gn_ref, y_ref, state_ref, decay_ref):
    L = RET_BLOCK
    head = pl.program_id(1)
    step = pl.program_id(2)
    lg = lg_ref[head]

    @pl.when(step == 0)
    def _():
        state_ref[...] = jnp.zeros_like(state_ref)
        n = lax.broadcasted_iota(jnp.int32, (L, L), 0)
        m = lax.broadcasted_iota(jnp.int32, (L, L), 1)
        dist = jnp.abs(n - m).astype(F32)
        decay_ref[...] = jnp.where((m // CHUNK) <= (n // CHUNK), jnp.exp(lg * dist), 0.0)

    idx = lax.broadcasted_iota(jnp.int32, (L, 1), 0).astype(F32)
    q_decay = jnp.exp(idx * lg)
    k_decay = jnp.exp((L - idx) * lg)
    block_decay = jnp.exp(jnp.full((1, RET_V_DIM), L, F32) * lg)
    decay = decay_ref[...]
    gn = gn_ref[...]
    state = state_ref[...]
    for r in range(0, q_ref.shape[0], L):
        rows = slice(r, r + L)
        q = q_ref[rows, :]
        k = k_ref[rows, :]
        v = v_ref[rows, :]
        s = lax.dot_general(q, k, (((1,), (1,)), ((), ())), preferred_element_type=F32)
        o = jnp.dot((s * decay).astype(BF16), v, preferred_element_type=F32)
        o = o + jnp.dot(q, state.astype(BF16), preferred_element_type=F32) * q_decay
        k_dec = (k.astype(F32) * k_decay).astype(BF16)
        upd = lax.dot_general(k_dec, v, (((0,), (0,)), ((), ())), preferred_element_type=F32)
        state = state * block_decay + upd

        mu = jnp.mean(o, -1, keepdims=True)
        d = o - mu
        var = jnp.mean(d * d, -1, keepdims=True)
        normed = d * lax.rsqrt(var + LN_EPS) * gn
        gate = g_ref[rows, :].astype(F32)
        y_ref[rows, :] = (gate * jax.nn.sigmoid(gate) * normed).astype(y_ref.dtype)
    state_ref[...] = state


def _ret_core(qk, vg, gn_g):
    ts = 1024
    ns = SEQ // ts
    log_gamma = jnp.log(1.0 - 2.0 ** (-5.0 - jnp.arange(RET_HEADS, dtype=F32)))
    return pl.pallas_call(
        _ret_core_kernel,
        grid=(BATCH, RET_HEADS, ns),
        in_specs=[pl.BlockSpec(memory_space=pltpu.SMEM),
                  pl.BlockSpec((ts, RET_QK_DIM), lambda b, h, c: (b * ns + c, h)),
                  pl.BlockSpec((ts, RET_QK_DIM), lambda b, h, c: (b * ns + c, RET_HEADS + h)),
                  pl.BlockSpec((ts, RET_V_DIM), lambda b, h, c: (b * ns + c, h)),
                  pl.BlockSpec((ts, RET_V_DIM), lambda b, h, c: (b * ns + c, RET_HEADS + h)),
                  pl.BlockSpec((1, RET_V_DIM), lambda b, h, c: (0, h))],
        out_specs=pl.BlockSpec((ts, RET_V_DIM), lambda b, h, c: (b * ns + c, h)),
        out_shape=jax.ShapeDtypeStruct((TOKENS, RET_V_WIDTH), BF16),
        scratch_shapes=[pltpu.VMEM((RET_QK_DIM, RET_V_DIM), F32),
                        pltpu.VMEM((RET_BLOCK, RET_BLOCK), F32)],
        compiler_params=_params("arbitrary", "arbitrary", "arbitrary"),
        name="ret_core",
    )(log_gamma, qk, qk, vg, vg, gn_g.reshape(1, RET_V_WIDTH))


def _rms(x, g):
    return x * lax.rsqrt(jnp.mean(x * x, -1, keepdims=True) + RMS_EPS) * g


def _mla_proj_kernel(x_ref, w_in_ref, qg_ref, kvg_ref, w_uq_ref, w_uk_ref, w_uv_ref,
                     c_ref, sp_ref, sn_ref, slab_ref, q_ref, k_ref, v_ref, slab_out_ref):
    _cast_slab(slab_ref, slab_out_ref)
    c = c_ref[...]
    sp = sp_ref[...]
    sn = sn_ref[...]
    proj = jnp.dot(x_ref[...], w_in_ref[...], preferred_element_type=F32)
    c_q = _rms(proj[:, :MLA_Q_RANK], qg_ref[...]).astype(BF16)
    c_kv = _rms(proj[:, MLA_Q_RANK:MLA_Q_RANK + MLA_KV_RANK], kvg_ref[...]).astype(BF16)
    k_rope = _rope_mla(proj[:, MLA_Q_RANK + MLA_KV_RANK:], c, sp, sn).astype(BF16)

    q = jnp.dot(c_q, w_uq_ref[...], preferred_element_type=F32)
    k_nope = jnp.dot(c_kv, w_uk_ref[...], preferred_element_type=F32)
    for hh in range(MLA_HEADS):
        lo = hh * MLA_QK_PAD
        q_ref[:, lo:lo + MLA_NOPE] = q[:, lo:lo + MLA_NOPE].astype(BF16)
        q_ref[:, lo + MLA_NOPE:lo + MLA_QK_PAD] = _rope_mla(
            q[:, lo + MLA_NOPE:lo + MLA_QK_PAD], c, sp, sn).astype(BF16)
        k_ref[:, lo:lo + MLA_NOPE] = k_nope[:, hh * MLA_NOPE:(hh + 1) * MLA_NOPE].astype(BF16)
        k_ref[:, lo + MLA_NOPE:lo + MLA_QK_PAD] = k_rope
    v_ref[...] = jnp.dot(c_kv, w_uv_ref[...], preferred_element_type=F32).astype(BF16)


def _mla_proj(xb, w_in, qg, kvg, w_uq, w_uk, w_uv, c, sp, sn, w_out, layer):
    tm = 256
    qk_w = MLA_HEADS * MLA_QK_PAD
    v_w = MLA_HEADS * MLA_V
    grid = (TOKENS // tm,)
    row = lambda w: pl.BlockSpec((tm, w), lambda i: (i, 0))
    res = lambda a: _single(a.shape, lambda i: (0, 0))
    slab_src, slab_dst, slab_shape = _slab_specs(w_out, layer, grid)
    return pl.pallas_call(
        _mla_proj_kernel,
        grid=grid,
        in_specs=[row(D_MODEL), res(w_in), res(qg), res(kvg), res(w_uq), res(w_uk), res(w_uv),
                  row(LANES), row(LANES), row(LANES), slab_src],
        out_specs=[row(qk_w), row(qk_w), row(v_w), slab_dst],
        out_shape=[jax.ShapeDtypeStruct((TOKENS, qk_w), BF16),
                   jax.ShapeDtypeStruct((TOKENS, qk_w), BF16),
                   jax.ShapeDtypeStruct((TOKENS, v_w), BF16), slab_shape],
        compiler_params=_params("arbitrary"),
        name="mla_proj",
    )(xb, w_in, qg, kvg, w_uq, w_uk, w_uv, c, sp, sn, w_out)


def _mla_attn_kernel(tq, q_ref, k_ref, v_ref, o_ref, s_ref):
    c_exp = (MLA_NOPE + MLA_ROPE) ** -0.5 * LOG2_E
    dims = (((1,), (1,)), ((), ()))
    row = lax.broadcasted_iota(jnp.int32, (tq, tq), 0)
    col = lax.broadcasted_iota(jnp.int32, (tq, tq), 1)
    visible = (row // CHUNK) >= (col // CHUNK)
    for c in range(SEQ // tq):
        lo = c * tq
        n = lo + tq
        buf = s_ref.at[c % 2]
        buf[:, 0:n] = lax.dot_general(q_ref[lo:n, :], k_ref[0:n, :], dims, preferred_element_type=F32) * c_exp
        buf[:, lo:n] = jnp.where(visible, buf[:, lo:n], NEG_INF)
        s = buf[:, 0:n]
        p = jnp.exp2(s - jnp.max(s, -1, keepdims=True))
        l = jnp.sum(p, -1, keepdims=True)
        acc = jnp.dot(p.astype(BF16), v_ref[0:n, :], preferred_element_type=F32)
        o_ref[lo:n, :] = (acc * (1.0 / l)).astype(o_ref.dtype)


def _mla_attn(q, k, v):
    tq = 256
    return pl.pallas_call(
        functools.partial(_mla_attn_kernel, tq),
        grid=(BATCH, MLA_HEADS),
        in_specs=[pl.BlockSpec((SEQ, MLA_QK_PAD), lambda b, h: (b, h)),
                  pl.BlockSpec((SEQ, MLA_QK_PAD), lambda b, h: (b, h)),
                  pl.BlockSpec((SEQ, MLA_V), lambda b, h: (b, h))],
        out_specs=pl.BlockSpec((SEQ, MLA_V), lambda b, h: (b, h)),
        out_shape=jax.ShapeDtypeStruct((TOKENS, MLA_HEADS * MLA_V), BF16),
        scratch_shapes=[pltpu.VMEM((2, tq, SEQ), F32)],
        compiler_params=_params("arbitrary", "arbitrary"),
        name="mla_attn",
    )(q, k, v)


def _xa_attn_kernel(q_ref, kv_ref, o_ref):
    for hh in range(XA_HEADS):
        cols = slice(hh * XA_DIM, (hh + 1) * XA_DIM)
        vcols = slice(D_MODEL + hh * XA_DIM, D_MODEL + (hh + 1) * XA_DIM)
        s = lax.dot_general(q_ref[:, cols], kv_ref[:, cols], (((1,), (1,)), ((), ())),
                            preferred_element_type=F32) * (XA_DIM ** -0.5)
        e = jnp.exp(s - jnp.max(s, -1, keepdims=True))
        p = e / jnp.sum(e, -1, keepdims=True)
        o_ref[:, cols] = jnp.dot(p.astype(BF16), kv_ref[:, vcols],
                                 preferred_element_type=F32).astype(o_ref.dtype)


def _xa_attn(q, kv):
    tq = 1024
    nq = SEQ // tq
    return pl.pallas_call(
        _xa_attn_kernel,
        grid=(BATCH, nq),
        in_specs=[pl.BlockSpec((tq, D_MODEL), lambda b, i: (b * nq + i, 0)),
                  pl.BlockSpec((MEM_LEN, 2 * D_MODEL), lambda b, i: (b, 0))],
        out_specs=pl.BlockSpec((tq, D_MODEL), lambda b, i: (b * nq + i, 0)),
        out_shape=jax.ShapeDtypeStruct((TOKENS, D_MODEL), BF16),
        compiler_params=_params("arbitrary", "arbitrary"),
        name="xa_attn",
    )(q, kv)


def _mla_weights(w_in, w_uq, w_ukv):
    pad = LANES - MLA_ROPE
    w_in = jnp.pad(w_in, ((0, 0), (0, pad)))
    w_uq = w_uq.reshape(MLA_Q_RANK, MLA_HEADS, MLA_NOPE + MLA_ROPE)
    w_uq = jnp.pad(w_uq, ((0, 0), (0, 0), (0, pad))).reshape(MLA_Q_RANK, MLA_HEADS * MLA_QK_PAD)
    w_ukv = w_ukv.reshape(MLA_KV_RANK, MLA_HEADS, MLA_NOPE + MLA_V)
    w_uk = w_ukv[:, :, :MLA_NOPE].reshape(MLA_KV_RANK, MLA_HEADS * MLA_NOPE)
    w_uv = w_ukv[:, :, MLA_NOPE:].reshape(MLA_KV_RANK, MLA_HEADS * MLA_V)
    return w_in.astype(BF16), w_uq.astype(BF16), w_uk.astype(BF16), w_uv.astype(BF16)


def kernel(x, mem, positions, ret_w_in, ret_gn_g, ret_w_out, mla_w_in, mla_q_norm_g, mla_w_uq, mla_kv_norm_g, mla_w_ukv, mla_w_out, xa_w_q, xa_w_kv, xa_w_out, ffn_w_up, ffn_conv_w, ffn_conv_b, ffn_w_down, ln_mix_g, ln_mix_b, ln_mem_g, ln_mem_b, ln_ffn_g, ln_ffn_b):
    h = x.reshape(TOKENS, D_MODEL)
    hb = h.astype(BF16)
    mem_b = mem.reshape(BATCH * MEM_LEN, D_MODEL).astype(BF16)
    cos_r, sin_r, c_m, sp_m, sn_m = _rope_tables(positions)

    for layer in range(DEPTH):
        j = layer // N_MIXERS
        if layer % N_MIXERS == 0:
            qk = _ret_qk(hb, ret_w_in, j, cos_r, sin_r)
            vg, w_out = _xres_matmul(hb, ret_w_in, j, 2 * RET_QK_WIDTH, 2 * RET_V_WIDTH, tn=512,
                                     name="ret_vg", cast=(ret_w_out, j))
            mix_in = _ret_core(qk, vg, ret_gn_g[j])
        else:
            w_in, w_uq, w_uk, w_uv = _mla_weights(mla_w_in[j], mla_w_uq[j], mla_w_ukv[j])
            q, k, v, w_out = _mla_proj(hb, w_in, mla_q_norm_g[j].reshape(1, MLA_Q_RANK),
                                       mla_kv_norm_g[j].reshape(1, MLA_KV_RANK), w_uq, w_uk, w_uv,
                                       c_m, sp_m, sn_m, mla_w_out, j)
            mix_in = _mla_attn(q, k, v)
        h, hb = _mm_res_ln(mix_in, w_out, h, ln_mix_g[layer], ln_mix_b[layer], tm=512, name="mix_out_ln")

        xq, xa_out_w = _xres_matmul(hb, xa_w_q, layer, 0, D_MODEL, tn=512, name="xa_q", cast=(xa_w_out, layer))
        xkv = _xres_matmul(mem_b, xa_w_kv, layer, 0, 2 * D_MODEL, tn=512, name="xa_kv")
        xo = _xa_attn(xq, xkv)
        h, hb = _mm_res_ln(xo, xa_out_w, h, ln_mem_g[layer], ln_mem_b[layer], tm=512, name="xa_out_ln")

        act, w_down = _ffn_up(hb, ffn_w_up, ffn_conv_w, ffn_conv_b, ffn_w_down, layer)
        h, hb = _mm_res_ln(act, w_down, h, ln_ffn_g[layer], ln_ffn_b[layer], tm=256, name="ffn_down_ln")

    return h.reshape(BATCH, SEQ, D_MODEL)
```

```python
import functools

import jax
import jax.numpy as jnp
from jax import lax
from jax.experimental import pallas as pl
from jax.experimental.pallas import tpu as pltpu

D_MODEL = 2048
BATCH = 4
SEQ = 2048
DEPTH = 4
CHUNK = 64
MEM_LEN = 256
N_MIXERS = 2

RET_HEADS = 8
RET_QK_DIM = D_MODEL // RET_HEADS
RET_V_DIM = 2 * D_MODEL // RET_HEADS
RET_QK_WIDTH = RET_HEADS * RET_QK_DIM
RET_V_WIDTH = RET_HEADS * RET_V_DIM

MLA_HEADS = 16
MLA_Q_RANK = 512
MLA_KV_RANK = 512
MLA_NOPE = 128
MLA_ROPE = 64
MLA_V = 128

XA_HEADS = 4
XA_DIM = D_MODEL // XA_HEADS

D_FF = 5632
CONV_W = 3

ROPE_BASE = 10000.0
LN_EPS = 1e-5
RMS_EPS = 1e-6
NEG_INF = -1e30
DEEPNORM_ALPHA = (2 * DEPTH) ** 0.25
LOG2_E = 1.4426950408889634
MLA_Q_SCALE = (MLA_NOPE + MLA_ROPE) ** -0.5 * LOG2_E
MLA_HEADS_PER_STEP = 2

TOKENS = BATCH * SEQ
LANES = 128
BF16_SUBLANES = 16
MLA_QK_PAD = 256
RET_BLOCK = 256
VMEM_LIMIT = 56 * 1024 * 1024

X_ROWS = 2 * SEQ
ROW_CHUNK = 1024
FFN_ROW_CHUNK = 512
LN_SUB = 256

F32 = jnp.float32
BF16 = jnp.bfloat16


def _params(*semantics):
    return pltpu.CompilerParams(dimension_semantics=semantics, vmem_limit_bytes=VMEM_LIMIT)


def _single(shape, index_map):
    return pl.BlockSpec(shape, index_map, pipeline_mode=pl.Buffered(1))


def _rope_tables_kernel(pos_ref, invf_ret_ref, invf_mla_ref, cos_r, sin_r, c_m, sp_m, sn_m):
    pos = pos_ref[...].astype(F32)
    ang = pos * invf_ret_ref[...]
    cos_r[...] = jnp.cos(ang)
    sin_r[...] = jnp.sin(ang)
    angm = pos * invf_mla_ref[...]
    lane = lax.broadcasted_iota(jnp.int32, angm.shape, 1)
    half = MLA_ROPE // 2
    c = jnp.cos(angm)
    s = jnp.sin(angm)
    c_m[...] = jnp.where(lane < MLA_ROPE, c, 0.0)
    sp_m[...] = jnp.where((lane >= half) & (lane < MLA_ROPE), s, 0.0)
    sn_m[...] = jnp.where(lane < half, -s, 0.0)


def _rope_tables(positions):
    tm = 1024
    pos = positions.reshape(TOKENS, 1)
    invf_ret = ROPE_BASE ** (-jnp.arange(0, RET_QK_DIM, 2, dtype=F32) / RET_QK_DIM)
    invf_mla = ROPE_BASE ** (-jnp.arange(0, MLA_ROPE, 2, dtype=F32) / MLA_ROPE)
    invf_mla = jnp.concatenate([invf_mla, invf_mla, jnp.zeros((LANES - MLA_ROPE,), F32)])
    row = pl.BlockSpec((tm, LANES), lambda i: (i, 0))
    const = pl.BlockSpec((1, LANES), lambda i: (0, 0))
    return pl.pallas_call(
        _rope_tables_kernel,
        grid=(TOKENS // tm,),
        in_specs=[pl.BlockSpec((tm, 1), lambda i: (i, 0)), const, const],
        out_specs=[row] * 5,
        out_shape=[jax.ShapeDtypeStruct((TOKENS, LANES), F32)] * 5,
        compiler_params=_params("arbitrary"),
        name="rope_tables",
    )(pos, invf_ret.reshape(1, LANES), invf_mla.reshape(1, LANES))


def _rope_mla(x, c, sp, sn):
    half = MLA_ROPE // 2
    return x * c + pltpu.roll(x, half, 1) * sp + pltpu.roll(x, LANES - half, 1) * sn


def _row_chunks(rows):
    chunk = min(ROW_CHUNK, rows)
    return [(r, chunk) for r in range(0, rows, chunk)]


def _cast_slab(src_ref, dst_ref):
    dst_ref[...] = src_ref[...].astype(BF16)


def _slab_specs(w_stack, layer, grid):
    k, n = w_stack.shape[1:]
    steps = 1
    for g in grid:
        steps *= g
    rows = k // steps
    assert rows * steps == k and rows % BF16_SUBLANES == 0
    if len(grid) == 1:
        flat = lambda i: i
    else:
        flat = lambda i, j: i * grid[1] + j
    src = pl.BlockSpec((None, rows, n), lambda *ids: (layer, flat(*ids), 0))
    dst = pl.BlockSpec((rows, n), lambda *ids: (flat(*ids), 0))
    return src, dst, jax.ShapeDtypeStruct((k, n), BF16)


def _xres_matmul_kernel(x_ref, w_ref, o_ref):
    w = w_ref[...].astype(BF16)
    for r, n in _row_chunks(x_ref.shape[0]):
        o_ref[r:r + n, :] = jnp.dot(x_ref[r:r + n, :], w, preferred_element_type=F32).astype(o_ref.dtype)


def _xres_matmul_cast_kernel(x_ref, w_ref, slab_ref, o_ref, slab_out_ref):
    _cast_slab(slab_ref, slab_out_ref)
    _xres_matmul_kernel(x_ref, w_ref, o_ref)


def _w_cols(w_stack, layer, col0, tn):
    k = w_stack.shape[1]
    first = col0 // tn
    return pl.BlockSpec((None, k, tn), lambda i, j: (layer, 0, first + j))


def _xres_matmul(x, w_stack, layer, col0, n, tn, name, cast=None):
    m, k = x.shape
    rows = min(X_ROWS, m)
    grid = (m // rows, n // tn)
    in_specs = [_single((rows, k), lambda i, j: (i, 0)), _w_cols(w_stack, layer, col0, tn)]
    out_specs = [pl.BlockSpec((rows, tn), lambda i, j: (i, j))]
    out_shape = [jax.ShapeDtypeStruct((m, n), BF16)]
    args = [x, w_stack]
    body = _xres_matmul_kernel
    if cast is not None:
        src, dst, shape = _slab_specs(cast[0], cast[1], grid)
        in_specs.append(src)
        out_specs.append(dst)
        out_shape.append(shape)
        args.append(cast[0])
        body = _xres_matmul_cast_kernel
    out = pl.pallas_call(
        body,
        grid=grid,
        in_specs=in_specs,
        out_specs=out_specs,
        out_shape=out_shape,
        compiler_params=_params("arbitrary", "arbitrary"),
        name=name,
    )(*args)
    return out if cast is not None else out[0]


def _ret_qk_kernel(tn, x_ref, w_ref, cos_ref, sin_ref, o_ref):
    j = pl.program_id(1)
    w = w_ref[...].astype(BF16)
    scale = jnp.where(j >= RET_QK_WIDTH // tn, RET_QK_DIM ** -0.5, 1.0).astype(F32)
    half = RET_QK_DIM // 2
    for r, n in _row_chunks(x_ref.shape[0]):
        acc = jnp.dot(x_ref[r:r + n, :], w, preferred_element_type=F32)
        cos = cos_ref[r:r + n, :]
        sin = sin_ref[r:r + n, :]
        for lo in range(0, tn, RET_QK_DIM):
            x1 = acc[:, lo:lo + half]
            x2 = acc[:, lo + half:lo + RET_QK_DIM]
            o_ref[r:r + n, lo:lo + half] = ((x1 * cos - x2 * sin) * scale).astype(o_ref.dtype)
            o_ref[r:r + n, lo + half:lo + RET_QK_DIM] = ((x2 * cos + x1 * sin) * scale).astype(o_ref.dtype)


def _ret_qk(xb, w_stack, layer, cos, sin):
    tn = 512
    n = 2 * RET_QK_WIDTH
    rows = X_ROWS
    return pl.pallas_call(
        functools.partial(_ret_qk_kernel, tn),
        grid=(TOKENS // rows, n // tn),
        in_specs=[_single((rows, D_MODEL), lambda i, j: (i, 0)),
                  _w_cols(w_stack, layer, 0, tn),
                  _single((rows, LANES), lambda i, j: (i, 0)),
                  _single((rows, LANES), lambda i, j: (i, 0))],
        out_specs=pl.BlockSpec((rows, tn), lambda i, j: (i, j)),
        out_shape=jax.ShapeDtypeStruct((TOKENS, n), BF16),
        compiler_params=_params("arbitrary", "arbitrary"),
        name="ret_qk",
    )(xb, w_stack, cos, sin)


def _ffn_up_kernel(x_ref, wg_ref, wv_ref, cwg_ref, cwv_ref, cbg_ref, cbv_ref, slab_ref,
                   o_ref, slab_out_ref, hg_ref, hv_ref):
    _cast_slab(slab_ref, slab_out_ref)
    halo = BF16_SUBLANES
    wg = wg_ref[...].astype(BF16)
    wv = wv_ref[...].astype(BF16)
    for r in range(0, x_ref.shape[0], FFN_ROW_CHUNK):
        n = FFN_ROW_CHUNK
        seq_start = r % SEQ == 0

        def conv(w, cw_ref, cb_ref, h_ref):
            if seq_start:
                h_ref[0:halo, :] = jnp.zeros((halo, h_ref.shape[1]), F32)
                h_ref[halo:, :] = jnp.dot(x_ref[r:r + n, :], w, preferred_element_type=F32)
            else:
                h_ref[...] = jnp.dot(x_ref[r - halo:r + n, :], w, preferred_element_type=F32)
            cw = cw_ref[...]
            out = (cw[2:3] * h_ref[halo:halo + n, :] + cw[1:2] * h_ref[halo - 1:halo - 1 + n, :]
                   + cw[0:1] * h_ref[halo - 2:halo - 2 + n, :])
            return out + cb_ref[...]

        gate = conv(wg, cwg_ref, cbg_ref, hg_ref)
        val = conv(wv, cwv_ref, cbv_ref, hv_ref)
        o_ref[r:r + n, :] = (gate * jax.nn.sigmoid(gate) * val).astype(o_ref.dtype)


def _ffn_up(xb, w_up, conv_w, conv_b, w_down, layer):
    tn = 256
    nj = D_FF // tn
    rows = X_ROWS
    grid = (TOKENS // rows, nj)
    conv_b = conv_b.reshape(DEPTH, 1, 2 * D_FF)
    taps = lambda depth, first: pl.BlockSpec((None, depth, tn), lambda i, j: (layer, 0, first + j))
    slab_src, slab_dst, slab_shape = _slab_specs(w_down, layer, grid)
    return pl.pallas_call(
        _ffn_up_kernel,
        grid=grid,
        in_specs=[_single((rows, D_MODEL), lambda i, j: (i, 0)),
                  _w_cols(w_up, layer, 0, tn), _w_cols(w_up, layer, D_FF, tn),
                  taps(CONV_W, 0), taps(CONV_W, nj), taps(1, 0), taps(1, nj), slab_src],
        out_specs=[pl.BlockSpec((rows, tn), lambda i, j: (i, j)), slab_dst],
        out_shape=[jax.ShapeDtypeStruct((TOKENS, D_FF), BF16), slab_shape],
        scratch_shapes=[pltpu.VMEM((BF16_SUBLANES + FFN_ROW_CHUNK, tn), F32)] * 2,
        compiler_params=_params("arbitrary", "arbitrary"),
        name="ffn_up",
    )(xb, w_up, w_up, conv_w, conv_w, conv_b, conv_b, w_down)


def _mm_res_ln_kernel(a_ref, w_ref, h_ref, g_ref, b_ref, of_ref, ob_ref):
    g = g_ref[...]
    b = b_ref[...]
    for r in range(0, a_ref.shape[0], LN_SUB):
        rows = slice(r, r + LN_SUB)
        y = DEEPNORM_ALPHA * h_ref[rows, :] + jnp.dot(a_ref[rows, :], w_ref[...], preferred_element_type=F32)
        mu = jnp.mean(y, -1, keepdims=True)
        d = y - mu
        var = jnp.mean(d * d, -1, keepdims=True)
        out = d * lax.rsqrt(var + LN_EPS) * g + b
        of_ref[rows, :] = out
        ob_ref[rows, :] = out.astype(BF16)


def _mm_res_ln(a, w, h, g, b, tm, name):
    m, k = a.shape
    n = w.shape[1]
    row = lambda width: pl.BlockSpec((tm, width), lambda i: (i, 0))
    vec = pl.BlockSpec((1, n), lambda i: (0, 0))
    return pl.pallas_call(
        _mm_res_ln_kernel,
        grid=(m // tm,),
        in_specs=[row(k), _single((k, n), lambda i: (0, 0)), row(n), vec, vec],
        out_specs=[row(n), row(n)],
        out_shape=[jax.ShapeDtypeStruct((m, n), F32), jax.ShapeDtypeStruct((m, n), BF16)],
        compiler_params=_params("arbitrary"),
        name=name,
    )(a, w, h, g.reshape(1, n), b.reshape(1, n))


def _ret_core_kernel(lg_ref, q_ref, k_ref, v_ref, g_ref, gn_ref, y_ref, state_ref, decay_ref):
    L = RET_BLOCK
    head = pl.program_id(1)
    step = pl.program_id(2)
    lg = lg_ref[head]

    @pl.when(step == 0)
    def _():
        state_ref[...] = jnp.zeros_like(state_ref)
        n = lax.broadcasted_iota(jnp.int32, (L, L), 0)
        m = lax.broadcasted_iota(jnp.int32, (L, L), 1)
        dist = jnp.abs(n - m).astype(F32)
        decay_ref[...] = jnp.where((m // CHUNK) <= (n // CHUNK), jnp.exp(lg * dist), 0.0)

    idx = lax.broadcasted_iota(jnp.int32, (L, 1), 0).astype(F32)
    q_decay = jnp.exp(idx * lg)
    k_decay = jnp.exp((L - idx) * lg)
    block_decay = jnp.exp(jnp.full((1, RET_V_DIM), L, F32) * lg)
    decay = decay_ref[...]
    gn = gn_ref[...]
    state = state_ref[...]
    for r in range(0, q_ref.shape[0], L):
        rows = slice(r, r + L)
        q = q_ref[rows, :]
        k = k_ref[rows, :]
        v = v_ref[rows, :]
        s = lax.dot_general(q, k, (((1,), (1,)), ((), ())), preferred_element_type=F32)
        o = jnp.dot((s * decay).astype(BF16), v, preferred_element_type=F32)
        o = o + jnp.dot(q, state.astype(BF16), preferred_element_type=F32) * q_decay
        k_dec = (k.astype(F32) * k_decay).astype(BF16)
        upd = lax.dot_general(k_dec, v, (((0,), (0,)), ((), ())), preferred_element_type=F32)
        state = state * block_decay + upd

        mu = jnp.mean(o, -1, keepdims=True)
        d = o - mu
        var = jnp.mean(d * d, -1, keepdims=True)
        normed = d * lax.rsqrt(var + LN_EPS) * gn
        gate = g_ref[rows, :].astype(F32)
        y_ref[rows, :] = (gate * jax.nn.sigmoid(gate) * normed).astype(y_ref.dtype)
    state_ref[...] = state


def _ret_core(qk, vg, gn_g):
    ts = SEQ
    ns = SEQ // ts
    log_gamma = jnp.log(1.0 - 2.0 ** (-5.0 - jnp.arange(RET_HEADS, dtype=F32)))
    return pl.pallas_call(
        _ret_core_kernel,
        grid=(BATCH, RET_HEADS, ns),
        in_specs=[pl.BlockSpec(memory_space=pltpu.SMEM),
                  pl.BlockSpec((ts, RET_QK_DIM), lambda b, h, c: (b * ns + c, h)),
                  pl.BlockSpec((ts, RET_QK_DIM), lambda b, h, c: (b * ns + c, RET_HEADS + h)),
                  pl.BlockSpec((ts, RET_V_DIM), lambda b, h, c: (b * ns + c, h)),
                  pl.BlockSpec((ts, RET_V_DIM), lambda b, h, c: (b * ns + c, RET_HEADS + h)),
                  pl.BlockSpec((1, RET_V_DIM), lambda b, h, c: (0, h))],
        out_specs=pl.BlockSpec((ts, RET_V_DIM), lambda b, h, c: (b * ns + c, h)),
        out_shape=jax.ShapeDtypeStruct((TOKENS, RET_V_WIDTH), BF16),
        scratch_shapes=[pltpu.VMEM((RET_QK_DIM, RET_V_DIM), F32),
                        pltpu.VMEM((RET_BLOCK, RET_BLOCK), F32)],
        compiler_params=_params("arbitrary", "arbitrary", "arbitrary"),
        name="ret_core",
    )(log_gamma, qk, qk, vg, vg, gn_g.reshape(1, RET_V_WIDTH))


def _rms(x, g):
    return x * lax.rsqrt(jnp.mean(x * x, -1, keepdims=True) + RMS_EPS) * g


def _mla_proj_kernel(x_ref, w_in_ref, qg_ref, kvg_ref, w_uq_ref, w_uk_ref, w_uv_ref,
                     c_ref, sp_ref, sn_ref, slab_ref, q_ref, k_ref, v_ref, slab_out_ref):
    _cast_slab(slab_ref, slab_out_ref)
    c = c_ref[...]
    sp = sp_ref[...]
    sn = sn_ref[...]
    proj = jnp.dot(x_ref[...], w_in_ref[...], preferred_element_type=F32)
    c_q = _rms(proj[:, :MLA_Q_RANK], qg_ref[...]).astype(BF16)
    c_kv = _rms(proj[:, MLA_Q_RANK:MLA_Q_RANK + MLA_KV_RANK], kvg_ref[...]).astype(BF16)
    k_rope = _rope_mla(proj[:, MLA_Q_RANK + MLA_KV_RANK:], c, sp, sn).astype(BF16)

    q = jnp.dot(c_q, w_uq_ref[...], preferred_element_type=F32)
    k_nope = jnp.dot(c_kv, w_uk_ref[...], preferred_element_type=F32)
    for hh in range(MLA_HEADS):
        lo = hh * MLA_QK_PAD
        q_ref[:, lo:lo + MLA_NOPE] = (q[:, lo:lo + MLA_NOPE] * MLA_Q_SCALE).astype(BF16)
        q_ref[:, lo + MLA_NOPE:lo + MLA_QK_PAD] = (_rope_mla(
            q[:, lo + MLA_NOPE:lo + MLA_QK_PAD], c, sp, sn) * MLA_Q_SCALE).astype(BF16)
        k_ref[:, lo:lo + MLA_NOPE] = k_nope[:, hh * MLA_NOPE:(hh + 1) * MLA_NOPE].astype(BF16)
        k_ref[:, lo + MLA_NOPE:lo + MLA_QK_PAD] = k_rope
    v_ref[...] = jnp.dot(c_kv, w_uv_ref[...], preferred_element_type=F32).astype(BF16)


def _mla_proj(xb, w_in, qg, kvg, w_uq, w_uk, w_uv, c, sp, sn, w_out, layer):
    tm = 256
    qk_w = MLA_HEADS * MLA_QK_PAD
    v_w = MLA_HEADS * MLA_V
    grid = (TOKENS // tm,)
    row = lambda w: pl.BlockSpec((tm, w), lambda i: (i, 0))
    res = lambda a: _single(a.shape, lambda i: (0, 0))
    slab_src, slab_dst, slab_shape = _slab_specs(w_out, layer, grid)
    return pl.pallas_call(
        _mla_proj_kernel,
        grid=grid,
        in_specs=[row(D_MODEL), res(w_in), res(qg), res(kvg), res(w_uq), res(w_uk), res(w_uv),
                  row(LANES), row(LANES), row(LANES), slab_src],
        out_specs=[row(qk_w), row(qk_w), row(v_w), slab_dst],
        out_shape=[jax.ShapeDtypeStruct((TOKENS, qk_w), BF16),
                   jax.ShapeDtypeStruct((TOKENS, qk_w), BF16),
                   jax.ShapeDtypeStruct((TOKENS, v_w), BF16), slab_shape],
        compiler_params=_params("arbitrary"),
        name="mla_proj",
    )(xb, w_in, qg, kvg, w_uq, w_uk, w_uv, c, sp, sn, w_out)


def _mla_attn_kernel(tq, q_ref, k_ref, v_ref, o_ref, s_ref, vx_ref):
    dims = (((1,), (1,)), ((), ()))
    row = lax.broadcasted_iota(jnp.int32, (tq, tq), 0)
    col = lax.broadcasted_iota(jnp.int32, (tq, tq), 1)
    visible = (row // CHUNK) >= (col // CHUNK)
    for hh in range(MLA_HEADS_PER_STEP):
        qk = slice(hh * MLA_QK_PAD, (hh + 1) * MLA_QK_PAD)
        vc = slice(hh * MLA_V, (hh + 1) * MLA_V)
        vx_ref[:, :MLA_V] = v_ref[:, vc]
        vx_ref[:, MLA_V:] = jnp.ones((SEQ, MLA_V), BF16)
        for c in range(SEQ // tq):
            lo = c * tq
            n = lo + tq
            buf = s_ref.at[c % 2]
            buf[:, 0:n] = lax.dot_general(q_ref[lo:n, qk], k_ref[0:n, qk], dims, preferred_element_type=F32)
            buf[:, lo:n] = jnp.where(visible, buf[:, lo:n], NEG_INF)
            s = buf[:, 0:n]
            p = jnp.exp2(s - jnp.max(s, -1, keepdims=True))
            acc = jnp.dot(p.astype(BF16), vx_ref[0:n, :], preferred_element_type=F32)
            o_ref[lo:n, vc] = (acc[:, :MLA_V] * (1.0 / acc[:, MLA_V:MLA_V + 1])).astype(o_ref.dtype)


def _mla_attn(q, k, v):
    tq = 256
    hp = MLA_HEADS_PER_STEP
    return pl.pallas_call(
        functools.partial(_mla_attn_kernel, tq),
        grid=(BATCH, MLA_HEADS // hp),
        in_specs=[pl.BlockSpec((SEQ, hp * MLA_QK_PAD), lambda b, h: (b, h)),
                  pl.BlockSpec((SEQ, hp * MLA_QK_PAD), lambda b, h: (b, h)),
                  pl.BlockSpec((SEQ, hp * MLA_V), lambda b, h: (b, h))],
        out_specs=pl.BlockSpec((SEQ, hp * MLA_V), lambda b, h: (b, h)),
        out_shape=jax.ShapeDtypeStruct((TOKENS, MLA_HEADS * MLA_V), BF16),
        scratch_shapes=[pltpu.VMEM((2, tq, SEQ), F32), pltpu.VMEM((SEQ, 2 * MLA_V), BF16)],
        compiler_params=_params("arbitrary", "arbitrary"),
        name="mla_attn",
    )(q, k, v)


def _xa_attn_kernel(q_ref, kv_ref, o_ref):
    for hh in range(XA_HEADS):
        cols = slice(hh * XA_DIM, (hh + 1) * XA_DIM)
        vcols = slice(D_MODEL + hh * XA_DIM, D_MODEL + (hh + 1) * XA_DIM)
        s = lax.dot_general(q_ref[:, cols], kv_ref[:, cols], (((1,), (1,)), ((), ())),
                            preferred_element_type=F32) * (XA_DIM ** -0.5)
        e = jnp.exp(s - jnp.max(s, -1, keepdims=True))
        p = e / jnp.sum(e, -1, keepdims=True)
        o_ref[:, cols] = jnp.dot(p.astype(BF16), kv_ref[:, vcols],
                                 preferred_element_type=F32).astype(o_ref.dtype)


def _xa_attn(q, kv):
    tq = 1024
    nq = SEQ // tq
    return pl.pallas_call(
        _xa_attn_kernel,
        grid=(BATCH, nq),
        in_specs=[pl.BlockSpec((tq, D_MODEL), lambda b, i: (b * nq + i, 0)),
                  pl.BlockSpec((MEM_LEN, 2 * D_MODEL), lambda b, i: (b, 0))],
        out_specs=pl.BlockSpec((tq, D_MODEL), lambda b, i: (b * nq + i, 0)),
        out_shape=jax.ShapeDtypeStruct((TOKENS, D_MODEL), BF16),
        compiler_params=_params("arbitrary", "arbitrary"),
        name="xa_attn",
    )(q, kv)


def _mla_weights(w_in, w_uq, w_ukv):
    pad = LANES - MLA_ROPE
    w_in = jnp.pad(w_in, ((0, 0), (0, pad)))
    w_uq = w_uq.reshape(MLA_Q_RANK, MLA_HEADS, MLA_NOPE + MLA_ROPE)
    w_uq = jnp.pad(w_uq, ((0, 0), (0, 0), (0, pad))).reshape(MLA_Q_RANK, MLA_HEADS * MLA_QK_PAD)
    w_ukv = w_ukv.reshape(MLA_KV_RANK, MLA_HEADS, MLA_NOPE + MLA_V)
    w_uk = w_ukv[:, :, :MLA_NOPE].reshape(MLA_KV_RANK, MLA_HEADS * MLA_NOPE)
    w_uv = w_ukv[:, :, MLA_NOPE:].reshape(MLA_KV_RANK, MLA_HEADS * MLA_V)
    return w_in.astype(BF16), w_uq.astype(BF16), w_uk.astype(BF16), w_uv.astype(BF16)


def kernel(x, mem, positions, ret_w_in, ret_gn_g, ret_w_out, mla_w_in, mla_q_norm_g, mla_w_uq, mla_kv_norm_g, mla_w_ukv, mla_w_out, xa_w_q, xa_w_kv, xa_w_out, ffn_w_up, ffn_conv_w, ffn_conv_b, ffn_w_down, ln_mix_g, ln_mix_b, ln_mem_g, ln_mem_b, ln_ffn_g, ln_ffn_b):
    h = x.reshape(TOKENS, D_MODEL)
    hb = h.astype(BF16)
    mem_b = mem.reshape(BATCH * MEM_LEN, D_MODEL).astype(BF16)
    cos_r, sin_r, c_m, sp_m, sn_m = _rope_tables(positions)

    for layer in range(DEPTH):
        j = layer // N_MIXERS
        if layer % N_MIXERS == 0:
            qk = _ret_qk(hb, ret_w_in, j, cos_r, sin_r)
            vg, w_out = _xres_matmul(hb, ret_w_in, j, 2 * RET_QK_WIDTH, 2 * RET_V_WIDTH, tn=512,
                                     name="ret_vg", cast=(ret_w_out, j))
            mix_in = _ret_core(qk, vg, ret_gn_g[j])
        else:
            w_in, w_uq, w_uk, w_uv = _mla_weights(mla_w_in[j], mla_w_uq[j], mla_w_ukv[j])
            q, k, v, w_out = _mla_proj(hb, w_in, mla_q_norm_g[j].reshape(1, MLA_Q_RANK),
                                       mla_kv_norm_g[j].reshape(1, MLA_KV_RANK), w_uq, w_uk, w_uv,
                                       c_m, sp_m, sn_m, mla_w_out, j)
            mix_in = _mla_attn(q, k, v)
        h, hb = _mm_res_ln(mix_in, w_out, h, ln_mix_g[layer], ln_mix_b[layer], tm=512, name="mix_out_ln")

        xq, xa_out_w = _xres_matmul(hb, xa_w_q, layer, 0, D_MODEL, tn=512, name="xa_q", cast=(xa_w_out, layer))
        xkv = _xres_matmul(mem_b, xa_w_kv, layer, 0, 2 * D_MODEL, tn=512, name="xa_kv")
        xo = _xa_attn(xq, xkv)
        h, hb = _mm_res_ln(xo, xa_out_w, h, ln_mem_g[layer], ln_mem_b[layer], tm=512, name="xa_out_ln")

        act, w_down = _ffn_up(hb, ffn_w_up, ffn_conv_w, ffn_conv_b, ffn_w_down, layer)
        h, hb = _mm_res_ln(act, w_down, h, ln_ffn_g[layer], ln_ffn_b[layer], tm=256, name="ffn_down_ln")

    return h.reshape(BATCH, SEQ, D_MODEL)
```

```python
import functools

import jax
import jax.numpy as jnp
from jax import lax
from jax.experimental import pallas as pl
from jax.experimental.pallas import tpu as pltpu

D_MODEL = 2048
BATCH = 4
SEQ = 2048
DEPTH = 4
CHUNK = 64
MEM_LEN = 256
N_MIXERS = 2

RET_HEADS = 8
RET_QK_DIM = D_MODEL // RET_HEADS
RET_V_DIM = 2 * D_MODEL // RET_HEADS
RET_QK_WIDTH = RET_HEADS * RET_QK_DIM
RET_V_WIDTH = RET_HEADS * RET_V_DIM

MLA_HEADS = 16
MLA_Q_RANK = 512
MLA_KV_RANK = 512
MLA_NOPE = 128
MLA_ROPE = 64
MLA_V = 128

XA_HEADS = 4
XA_DIM = D_MODEL // XA_HEADS

D_FF = 5632
CONV_W = 3

ROPE_BASE = 10000.0
LN_EPS = 1e-5
RMS_EPS = 1e-6
NEG_INF = -1e30
DEEPNORM_ALPHA = (2 * DEPTH) ** 0.25
LOG2_E = 1.4426950408889634
MLA_Q_SCALE = (MLA_NOPE + MLA_ROPE) ** -0.5 * LOG2_E
MLA_HEADS_PER_STEP = 2

TOKENS = BATCH * SEQ
LANES = 128
BF16_SUBLANES = 16
MLA_QK_PAD = 256
RET_BLOCK = 256
VMEM_LIMIT = 56 * 1024 * 1024

X_ROWS = 2 * SEQ
ROW_CHUNK = 1024
FFN_ROW_CHUNK = 512
LN_SUB = 256

F32 = jnp.float32
BF16 = jnp.bfloat16


def _params(*semantics):
    return pltpu.CompilerParams(dimension_semantics=semantics, vmem_limit_bytes=VMEM_LIMIT)


def _single(shape, index_map):
    return pl.BlockSpec(shape, index_map, pipeline_mode=pl.Buffered(1))


def _rope_tables_kernel(pos_ref, invf_ret_ref, invf_mla_ref, cos_r, sin_r, c_m, sp_m, sn_m):
    pos = pos_ref[...].astype(F32)
    ang = pos * invf_ret_ref[...]
    cos_r[...] = jnp.cos(ang)
    sin_r[...] = jnp.sin(ang)
    angm = pos * invf_mla_ref[...]
    lane = lax.broadcasted_iota(jnp.int32, angm.shape, 1)
    half = MLA_ROPE // 2
    c = jnp.cos(angm)
    s = jnp.sin(angm)
    c_m[...] = jnp.where(lane < MLA_ROPE, c, 0.0)
    sp_m[...] = jnp.where((lane >= half) & (lane < MLA_ROPE), s, 0.0)
    sn_m[...] = jnp.where(lane < half, -s, 0.0)


def _rope_tables(positions):
    tm = 1024
    pos = positions.reshape(TOKENS, 1)
    invf_ret = ROPE_BASE ** (-jnp.arange(0, RET_QK_DIM, 2, dtype=F32) / RET_QK_DIM)
    invf_mla = ROPE_BASE ** (-jnp.arange(0, MLA_ROPE, 2, dtype=F32) / MLA_ROPE)
    invf_mla = jnp.concatenate([invf_mla, invf_mla, jnp.zeros((LANES - MLA_ROPE,), F32)])
    row = pl.BlockSpec((tm, LANES), lambda i: (i, 0))
    const = pl.BlockSpec((1, LANES), lambda i: (0, 0))
    return pl.pallas_call(
        _rope_tables_kernel,
        grid=(TOKENS // tm,),
        in_specs=[pl.BlockSpec((tm, 1), lambda i: (i, 0)), const, const],
        out_specs=[row] * 5,
        out_shape=[jax.ShapeDtypeStruct((TOKENS, LANES), F32)] * 5,
        compiler_params=_params("arbitrary"),
        name="rope_tables",
    )(pos, invf_ret.reshape(1, LANES), invf_mla.reshape(1, LANES))


def _rope_mla(x, c, sp, sn):
    half = MLA_ROPE // 2
    return x * c + pltpu.roll(x, half, 1) * sp + pltpu.roll(x, LANES - half, 1) * sn


def _row_chunks(rows):
    chunk = min(ROW_CHUNK, rows)
    return [(r, chunk) for r in range(0, rows, chunk)]


def _cast_slab(src_ref, dst_ref):
    dst_ref[...] = src_ref[...].astype(BF16)


def _slab_specs(w_stack, layer, grid):
    k, n = w_stack.shape[1:]
    steps = 1
    for g in grid:
        steps *= g
    rows = k // steps
    assert rows * steps == k and rows % BF16_SUBLANES == 0
    if len(grid) == 1:
        flat = lambda i: i
    else:
        flat = lambda i, j: i * grid[1] + j
    src = pl.BlockSpec((None, rows, n), lambda *ids: (layer, flat(*ids), 0))
    dst = pl.BlockSpec((rows, n), lambda *ids: (flat(*ids), 0))
    return src, dst, jax.ShapeDtypeStruct((k, n), BF16)


def _xres_matmul_kernel(x_ref, w_ref, o_ref):
    w = w_ref[...].astype(BF16)
    for r, n in _row_chunks(x_ref.shape[0]):
        o_ref[r:r + n, :] = jnp.dot(x_ref[r:r + n, :], w, preferred_element_type=F32).astype(o_ref.dtype)


def _xres_matmul_cast_kernel(x_ref, w_ref, slab_ref, o_ref, slab_out_ref):
    _cast_slab(slab_ref, slab_out_ref)
    _xres_matmul_kernel(x_ref, w_ref, o_ref)


def _w_cols(w_stack, layer, col0, tn):
    k = w_stack.shape[1]
    first = col0 // tn
    return pl.BlockSpec((None, k, tn), lambda i, j: (layer, 0, first + j))


def _xres_matmul(x, w_stack, layer, col0, n, tn, name, cast=None):
    m, k = x.shape
    rows = min(X_ROWS, m)
    grid = (m // rows, n // tn)
    in_specs = [_single((rows, k), lambda i, j: (i, 0)), _w_cols(w_stack, layer, col0, tn)]
    out_specs = [pl.BlockSpec((rows, tn), lambda i, j: (i, j))]
    out_shape = [jax.ShapeDtypeStruct((m, n), BF16)]
    args = [x, w_stack]
    body = _xres_matmul_kernel
    if cast is not None:
        src, dst, shape = _slab_specs(cast[0], cast[1], grid)
        in_specs.append(src)
        out_specs.append(dst)
        out_shape.append(shape)
        args.append(cast[0])
        body = _xres_matmul_cast_kernel
    out = pl.pallas_call(
        body,
        grid=grid,
        in_specs=in_specs,
        out_specs=out_specs,
        out_shape=out_shape,
        compiler_params=_params("arbitrary", "arbitrary"),
        name=name,
    )(*args)
    return out if cast is not None else out[0]


def _ret_qk_kernel(tn, x_ref, w_ref, cos_ref, sin_ref, o_ref):
    j = pl.program_id(1)
    w = w_ref[...].astype(BF16)
    scale = jnp.where(j >= RET_QK_WIDTH // tn, RET_QK_DIM ** -0.5, 1.0).astype(F32)
    half = RET_QK_DIM // 2
    for r, n in _row_chunks(x_ref.shape[0]):
        acc = jnp.dot(x_ref[r:r + n, :], w, preferred_element_type=F32)
        cos = cos_ref[r:r + n, :]
        sin = sin_ref[r:r + n, :]
        for lo in range(0, tn, RET_QK_DIM):
            x1 = acc[:, lo:lo + half]
            x2 = acc[:, lo + half:lo + RET_QK_DIM]
            o_ref[r:r + n, lo:lo + half] = ((x1 * cos - x2 * sin) * scale).astype(o_ref.dtype)
            o_ref[r:r + n, lo + half:lo + RET_QK_DIM] = ((x2 * cos + x1 * sin) * scale).astype(o_ref.dtype)


def _ret_qk(xb, w_stack, layer, cos, sin):
    tn = 512
    n = 2 * RET_QK_WIDTH
    rows = X_ROWS
    return pl.pallas_call(
        functools.partial(_ret_qk_kernel, tn),
        grid=(TOKENS // rows, n // tn),
        in_specs=[_single((rows, D_MODEL), lambda i, j: (i, 0)),
                  _w_cols(w_stack, layer, 0, tn),
                  _single((rows, LANES), lambda i, j: (i, 0)),
                  _single((rows, LANES), lambda i, j: (i, 0))],
        out_specs=pl.BlockSpec((rows, tn), lambda i, j: (i, j)),
        out_shape=jax.ShapeDtypeStruct((TOKENS, n), BF16),
        compiler_params=_params("arbitrary", "arbitrary"),
        name="ret_qk",
    )(xb, w_stack, cos, sin)


def _ffn_up_kernel(x_ref, wg_ref, wv_ref, cwg_ref, cwv_ref, cbg_ref, cbv_ref, slab_ref,
                   o_ref, slab_out_ref, hg_ref, hv_ref):
    _cast_slab(slab_ref, slab_out_ref)
    halo = BF16_SUBLANES
    wg = wg_ref[...].astype(BF16)
    wv = wv_ref[...].astype(BF16)
    for r in range(0, x_ref.shape[0], FFN_ROW_CHUNK):
        n = FFN_ROW_CHUNK
        seq_start = r % SEQ == 0

        def conv(w, cw_ref, cb_ref, h_ref):
            if seq_start:
                h_ref[0:halo, :] = jnp.zeros((halo, h_ref.shape[1]), F32)
                h_ref[halo:, :] = jnp.dot(x_ref[r:r + n, :], w, preferred_element_type=F32)
            else:
                h_ref[...] = jnp.dot(x_ref[r - halo:r + n, :], w, preferred_element_type=F32)
            cw = cw_ref[...]
            out = (cw[2:3] * h_ref[halo:halo + n, :] + cw[1:2] * h_ref[halo - 1:halo - 1 + n, :]
                   + cw[0:1] * h_ref[halo - 2:halo - 2 + n, :])
            return out + cb_ref[...]

        gate = conv(wg, cwg_ref, cbg_ref, hg_ref)
        val = conv(wv, cwv_ref, cbv_ref, hv_ref)
        o_ref[r:r + n, :] = (gate * jax.nn.sigmoid(gate) * val).astype(o_ref.dtype)


def _ffn_up(xb, w_up, conv_w, conv_b, w_down, layer):
    tn = 512
    nj = D_FF // tn
    rows = X_ROWS
    grid = (TOKENS // rows, nj)
    conv_b = conv_b.reshape(DEPTH, 1, 2 * D_FF)
    taps = lambda depth, first: pl.BlockSpec((None, depth, tn), lambda i, j: (layer, 0, first + j))
    slab_src, slab_dst, slab_shape = _slab_specs(w_down, layer, grid)
    return pl.pallas_call(
        _ffn_up_kernel,
        grid=grid,
        in_specs=[_single((rows, D_MODEL), lambda i, j: (i, 0)),
                  _w_cols(w_up, layer, 0, tn), _w_cols(w_up, layer, D_FF, tn),
                  taps(CONV_W, 0), taps(CONV_W, nj), taps(1, 0), taps(1, nj), slab_src],
        out_specs=[pl.BlockSpec((rows, tn), lambda i, j: (i, j)), slab_dst],
        out_shape=[jax.ShapeDtypeStruct((TOKENS, D_FF), BF16), slab_shape],
        scratch_shapes=[pltpu.VMEM((BF16_SUBLANES + FFN_ROW_CHUNK, tn), F32)] * 2,
        compiler_params=_params("arbitrary", "arbitrary"),
        name="ffn_up",
    )(xb, w_up, w_up, conv_w, conv_w, conv_b, conv_b, w_down)


def _mm_res_ln_kernel(a_ref, w_ref, h_ref, g_ref, b_ref, of_ref, ob_ref):
    g = g_ref[...]
    b = b_ref[...]
    for r in range(0, a_ref.shape[0], LN_SUB):
        rows = slice(r, r + LN_SUB)
        y = DEEPNORM_ALPHA * h_ref[rows, :] + jnp.dot(a_ref[rows, :], w_ref[...], preferred_element_type=F32)
        mu = jnp.mean(y, -1, keepdims=True)
        d = y - mu
        var = jnp.mean(d * d, -1, keepdims=True)
        out = d * lax.rsqrt(var + LN_EPS) * g + b
        of_ref[rows, :] = out
        ob_ref[rows, :] = out.astype(BF16)


def _mm_res_ln(a, w, h, g, b, tm, name):
    m, k = a.shape
    n = w.shape[1]
    row = lambda width: pl.BlockSpec((tm, width), lambda i: (i, 0))
    vec = pl.BlockSpec((1, n), lambda i: (0, 0))
    return pl.pallas_call(
        _mm_res_ln_kernel,
        grid=(m // tm,),
        in_specs=[row(k), _single((k, n), lambda i: (0, 0)), row(n), vec, vec],
        out_specs=[row(n), row(n)],
        out_shape=[jax.ShapeDtypeStruct((m, n), F32), jax.ShapeDtypeStruct((m, n), BF16)],
        compiler_params=_params("arbitrary"),
        name=name,
    )(a, w, h, g.reshape(1, n), b.reshape(1, n))


def _ret_core_kernel(lg_ref, q_ref, k_ref, v_ref, g_ref, gn_ref, y_ref, state_ref, decay_ref):
    L = RET_BLOCK
    head = pl.program_id(1)
    step = pl.program_id(2)
    lg = lg_ref[head]

    @pl.when(step == 0)
    def _():
        state_ref[...] = jnp.zeros_like(state_ref)
        n = lax.broadcasted_iota(jnp.int32, (L, L), 0)
        m = lax.broadcasted_iota(jnp.int32, (L, L), 1)
        dist = jnp.abs(n - m).astype(F32)
        decay_ref[...] = jnp.where((m // CHUNK) <= (n // CHUNK), jnp.exp(lg * dist), 0.0)

    idx = lax.broadcasted_iota(jnp.int32, (L, 1), 0).astype(F32)
    q_decay = jnp.exp(idx * lg)
    k_decay = jnp.exp((L - idx) * lg)
    block_decay = jnp.exp(jnp.full((1, RET_V_DIM), L, F32) * lg)
    decay = decay_ref[...]
    gn = gn_ref[...]
    state = state_ref[...]
    for r in range(0, q_ref.shape[0], L):
        rows = slice(r, r + L)
        q = q_ref[rows, :]
        k = k_ref[rows, :]
        v = v_ref[rows, :]
        s = lax.dot_general(q, k, (((1,), (1,)), ((), ())), preferred_element_type=F32)
        o = jnp.dot((s * decay).astype(BF16), v, preferred_element_type=F32)
        o = o + jnp.dot(q, state.astype(BF16), preferred_element_type=F32) * q_decay
        k_dec = (k.astype(F32) * k_decay).astype(BF16)
        upd = lax.dot_general(k_dec, v, (((0,), (0,)), ((), ())), preferred_element_type=F32)
        state = state * block_decay + upd

        mu = jnp.mean(o, -1, keepdims=True)
        d = o - mu
        var = jnp.mean(d * d, -1, keepdims=True)
        normed = d * lax.rsqrt(var + LN_EPS) * gn
        gate = g_ref[rows, :].astype(F32)
        y_ref[rows, :] = (gate * jax.nn.sigmoid(gate) * normed).astype(y_ref.dtype)
    state_ref[...] = state


def _ret_core(qk, vg, gn_g):
    ts = SEQ
    ns = SEQ // ts
    log_gamma = jnp.log(1.0 - 2.0 ** (-5.0 - jnp.arange(RET_HEADS, dtype=F32)))
    return pl.pallas_call(
        _ret_core_kernel,
        grid=(BATCH, RET_HEADS, ns),
        in_specs=[pl.BlockSpec(memory_space=pltpu.SMEM),
                  pl.BlockSpec((ts, RET_QK_DIM), lambda b, h, c: (b * ns + c, h)),
                  pl.BlockSpec((ts, RET_QK_DIM), lambda b, h, c: (b * ns + c, RET_HEADS + h)),
                  pl.BlockSpec((ts, RET_V_DIM), lambda b, h, c: (b * ns + c, h)),
                  pl.BlockSpec((ts, RET_V_DIM), lambda b, h, c: (b * ns + c, RET_HEADS + h)),
                  pl.BlockSpec((1, RET_V_DIM), lambda b, h, c: (0, h))],
        out_specs=pl.BlockSpec((ts, RET_V_DIM), lambda b, h, c: (b * ns + c, h)),
        out_shape=jax.ShapeDtypeStruct((TOKENS, RET_V_WIDTH), BF16),
        scratch_shapes=[pltpu.VMEM((RET_QK_DIM, RET_V_DIM), F32),
                        pltpu.VMEM((RET_BLOCK, RET_BLOCK), F32)],
        compiler_params=_params("arbitrary", "arbitrary", "arbitrary"),
        name="ret_core",
    )(log_gamma, qk, qk, vg, vg, gn_g.reshape(1, RET_V_WIDTH))


def _rms(x, g):
    return x * lax.rsqrt(jnp.mean(x * x, -1, keepdims=True) + RMS_EPS) * g


def _mla_proj_kernel(x_ref, w_in_ref, qg_ref, kvg_ref, w_uq_ref, w_uk_ref, w_uv_ref,
                     c_ref, sp_ref, sn_ref, slab_ref, q_ref, k_ref, v_ref, slab_out_ref):
    _cast_slab(slab_ref, slab_out_ref)
    c = c_ref[...]
    sp = sp_ref[...]
    sn = sn_ref[...]
    proj = jnp.dot(x_ref[...], w_in_ref[...], preferred_element_type=F32)
    c_q = _rms(proj[:, :MLA_Q_RANK], qg_ref[...]).astype(BF16)
    c_kv = _rms(proj[:, MLA_Q_RANK:MLA_Q_RANK + MLA_KV_RANK], kvg_ref[...]).astype(BF16)
    k_rope = _rope_mla(proj[:, MLA_Q_RANK + MLA_KV_RANK:], c, sp, sn).astype(BF16)

    q = jnp.dot(c_q, w_uq_ref[...], preferred_element_type=F32)
    k_nope = jnp.dot(c_kv, w_uk_ref[...], preferred_element_type=F32)
    for hh in range(MLA_HEADS):
        lo = hh * MLA_QK_PAD
        q_ref[:, lo:lo + MLA_NOPE] = (q[:, lo:lo + MLA_NOPE] * MLA_Q_SCALE).astype(BF16)
        q_ref[:, lo + MLA_NOPE:lo + MLA_QK_PAD] = (_rope_mla(
            q[:, lo + MLA_NOPE:lo + MLA_QK_PAD], c, sp, sn) * MLA_Q_SCALE).astype(BF16)
        k_ref[:, lo:lo + MLA_NOPE] = k_nope[:, hh * MLA_NOPE:(hh + 1) * MLA_NOPE].astype(BF16)
        k_ref[:, lo + MLA_NOPE:lo + MLA_QK_PAD] = k_rope
    v_ref[...] = jnp.dot(c_kv, w_uv_ref[...], preferred_element_type=F32).astype(BF16)


def _mla_proj(xb, w_in, qg, kvg, w_uq, w_uk, w_uv, c, sp, sn, w_out, layer):
    tm = 512
    qk_w = MLA_HEADS * MLA_QK_PAD
    v_w = MLA_HEADS * MLA_V
    grid = (TOKENS // tm,)
    row = lambda w: pl.BlockSpec((tm, w), lambda i: (i, 0))
    res = lambda a: _single(a.shape, lambda i: (0, 0))
    slab_src, slab_dst, slab_shape = _slab_specs(w_out, layer, grid)
    return pl.pallas_call(
        _mla_proj_kernel,
        grid=grid,
        in_specs=[row(D_MODEL), res(w_in), res(qg), res(kvg), res(w_uq), res(w_uk), res(w_uv),
                  row(LANES), row(LANES), row(LANES), slab_src],
        out_specs=[row(qk_w), row(qk_w), row(v_w), slab_dst],
        out_shape=[jax.ShapeDtypeStruct((TOKENS, qk_w), BF16),
                   jax.ShapeDtypeStruct((TOKENS, qk_w), BF16),
                   jax.ShapeDtypeStruct((TOKENS, v_w), BF16), slab_shape],
        compiler_params=_params("arbitrary"),
        name="mla_proj",
    )(xb, w_in, qg, kvg, w_uq, w_uk, w_uv, c, sp, sn, w_out)


def _mla_attn_kernel(tq, q_ref, k_ref, v_ref, o_ref, s_ref):
    dims = (((1,), (1,)), ((), ()))
    row = lax.broadcasted_iota(jnp.int32, (tq, tq), 0)
    col = lax.broadcasted_iota(jnp.int32, (tq, tq), 1)
    visible = (row // CHUNK) >= (col // CHUNK)
    for hh in range(MLA_HEADS_PER_STEP):
        qk = slice(hh * MLA_QK_PAD, (hh + 1) * MLA_QK_PAD)
        vc = slice(hh * MLA_V, (hh + 1) * MLA_V)
        for c in range(SEQ // tq):
            lo = c * tq
            n = lo + tq
            buf = s_ref.at[c % 2]
            buf[:, 0:n] = lax.dot_general(q_ref[lo:n, qk], k_ref[0:n, qk], dims, preferred_element_type=F32)
            buf[:, lo:n] = jnp.where(visible, buf[:, lo:n], NEG_INF)
            s = buf[:, 0:n]
            p = jnp.exp2(s - jnp.max(s, -1, keepdims=True))
            l = jnp.sum(p, -1, keepdims=True)
            acc = jnp.dot(p.astype(BF16), v_ref[0:n, vc], preferred_element_type=F32)
            o_ref[lo:n, vc] = (acc * (1.0 / l)).astype(o_ref.dtype)


def _mla_attn(q, k, v):
    tq = 256
    hp = MLA_HEADS_PER_STEP
    return pl.pallas_call(
        functools.partial(_mla_attn_kernel, tq),
        grid=(BATCH, MLA_HEADS // hp),
        in_specs=[pl.BlockSpec((SEQ, hp * MLA_QK_PAD), lambda b, h: (b, h)),
                  pl.BlockSpec((SEQ, hp * MLA_QK_PAD), lambda b, h: (b, h)),
                  pl.BlockSpec((SEQ, hp * MLA_V), lambda b, h: (b, h))],
        out_specs=pl.BlockSpec((SEQ, hp * MLA_V), lambda b, h: (b, h)),
        out_shape=jax.ShapeDtypeStruct((TOKENS, MLA_HEADS * MLA_V), BF16),
        scratch_shapes=[pltpu.VMEM((2, tq, SEQ), F32)],
        compiler_params=_params("arbitrary", "arbitrary"),
        name="mla_attn",
    )(q, k, v)


def _xa_attn_kernel(q_ref, kv_ref, o_ref):
    for hh in range(XA_HEADS):
        cols = slice(hh * XA_DIM, (hh + 1) * XA_DIM)
        vcols = slice(D_MODEL + hh * XA_DIM, D_MODEL + (hh + 1) * XA_DIM)
        s = lax.dot_general(q_ref[:, cols], kv_ref[:, cols], (((1,), (1,)), ((), ())),
                            preferred_element_type=F32) * (XA_DIM ** -0.5)
        e = jnp.exp(s - jnp.max(s, -1, keepdims=True))
        p = e / jnp.sum(e, -1, keepdims=True)
        o_ref[:, cols] = jnp.dot(p.astype(BF16), kv_ref[:, vcols],
                                 preferred_element_type=F32).astype(o_ref.dtype)


def _xa_attn(q, kv):
    tq = 1024
    nq = SEQ // tq
    return pl.pallas_call(
        _xa_attn_kernel,
        grid=(BATCH, nq),
        in_specs=[pl.BlockSpec((tq, D_MODEL), lambda b, i: (b * nq + i, 0)),
                  pl.BlockSpec((MEM_LEN, 2 * D_MODEL), lambda b, i: (b, 0))],
        out_specs=pl.BlockSpec((tq, D_MODEL), lambda b, i: (b * nq + i, 0)),
        out_shape=jax.ShapeDtypeStruct((TOKENS, D_MODEL), BF16),
        compiler_params=_params("arbitrary", "arbitrary"),
        name="xa_attn",
    )(q, kv)


def _mla_weights(w_in, w_uq, w_ukv):
    pad = LANES - MLA_ROPE
    w_in = jnp.pad(w_in, ((0, 0), (0, pad)))
    w_uq = w_uq.reshape(MLA_Q_RANK, MLA_HEADS, MLA_NOPE + MLA_ROPE)
    w_uq = jnp.pad(w_uq, ((0, 0), (0, 0), (0, pad))).reshape(MLA_Q_RANK, MLA_HEADS * MLA_QK_PAD)
    w_ukv = w_ukv.reshape(MLA_KV_RANK, MLA_HEADS, MLA_NOPE + MLA_V)
    w_uk = w_ukv[:, :, :MLA_NOPE].reshape(MLA_KV_RANK, MLA_HEADS * MLA_NOPE)
    w_uv = w_ukv[:, :, MLA_NOPE:].reshape(MLA_KV_RANK, MLA_HEADS * MLA_V)
    return w_in.astype(BF16), w_uq.astype(BF16), w_uk.astype(BF16), w_uv.astype(BF16)


def kernel(x, mem, positions, ret_w_in, ret_gn_g, ret_w_out, mla_w_in, mla_q_norm_g, mla_w_uq, mla_kv_norm_g, mla_w_ukv, mla_w_out, xa_w_q, xa_w_kv, xa_w_out, ffn_w_up, ffn_conv_w, ffn_conv_b, ffn_w_down, ln_mix_g, ln_mix_b, ln_mem_g, ln_mem_b, ln_ffn_g, ln_ffn_b):
    h = x.reshape(TOKENS, D_MODEL)
    hb = h.astype(BF16)
    mem_b = mem.reshape(BATCH * MEM_LEN, D_MODEL).astype(BF16)
    cos_r, sin_r, c_m, sp_m, sn_m = _rope_tables(positions)

    for layer in range(DEPTH):
        j = layer // N_MIXERS
        if layer % N_MIXERS == 0:
            qk = _ret_qk(hb, ret_w_in, j, cos_r, sin_r)
            vg, w_out = _xres_matmul(hb, ret_w_in, j, 2 * RET_QK_WIDTH, 2 * RET_V_WIDTH, tn=512,
                                     name="ret_vg", cast=(ret_w_out, j))
            mix_in = _ret_core(qk, vg, ret_gn_g[j])
        else:
            w_in, w_uq, w_uk, w_uv = _mla_weights(mla_w_in[j], mla_w_uq[j], mla_w_ukv[j])
            q, k, v, w_out = _mla_proj(hb, w_in, mla_q_norm_g[j].reshape(1, MLA_Q_RANK),
                                       mla_kv_norm_g[j].reshape(1, MLA_KV_RANK), w_uq, w_uk, w_uv,
                                       c_m, sp_m, sn_m, mla_w_out, j)
            mix_in = _mla_attn(q, k, v)
        h, hb = _mm_res_ln(mix_in, w_out, h, ln_mix_g[layer], ln_mix_b[layer], tm=512, name="mix_out_ln")

        xq, xa_out_w = _xres_matmul(hb, xa_w_q, layer, 0, D_MODEL, tn=512, name="xa_q", cast=(xa_w_out, layer))
        xkv = _xres_matmul(mem_b, xa_w_kv, layer, 0, 2 * D_MODEL, tn=512, name="xa_kv")
        xo = _xa_attn(xq, xkv)
        h, hb = _mm_res_ln(xo, xa_out_w, h, ln_mem_g[layer], ln_mem_b[layer], tm=512, name="xa_out_ln")

        act, w_down = _ffn_up(hb, ffn_w_up, ffn_conv_w, ffn_conv_b, ffn_w_down, layer)
        h, hb = _mm_res_ln(act, w_down, h, ln_ffn_g[layer], ln_ffn_b[layer], tm=256, name="ffn_down_ln")

    return h.reshape(BATCH, SEQ, D_MODEL)
```

```python
import functools

import jax
import jax.numpy as jnp
from jax import lax
from jax.experimental import pallas as pl
from jax.experimental.pallas import tpu as pltpu

D_MODEL = 2048
BATCH = 4
SEQ = 2048
DEPTH = 4
CHUNK = 64
MEM_LEN = 256
N_MIXERS = 2

RET_HEADS = 8
RET_QK_DIM = D_MODEL // RET_HEADS
RET_V_DIM = 2 * D_MODEL // RET_HEADS
RET_QK_WIDTH = RET_HEADS * RET_QK_DIM
RET_V_WIDTH = RET_HEADS * RET_V_DIM

MLA_HEADS = 16
MLA_Q_RANK = 512
MLA_KV_RANK = 512
MLA_NOPE = 128
MLA_ROPE = 64
MLA_V = 128

XA_HEADS = 4
XA_DIM = D_MODEL // XA_HEADS

D_FF = 5632
CONV_W = 3

ROPE_BASE = 10000.0
LN_EPS = 1e-5
RMS_EPS = 1e-6
NEG_INF = -1e30
DEEPNORM_ALPHA = (2 * DEPTH) ** 0.25
LOG2_E = 1.4426950408889634
MLA_Q_SCALE = (MLA_NOPE + MLA_ROPE) ** -0.5 * LOG2_E
MLA_HEADS_PER_STEP = 2

TOKENS = BATCH * SEQ
LANES = 128
BF16_SUBLANES = 16
MLA_QK_PAD = 256
RET_BLOCK = 256
VMEM_LIMIT = 56 * 1024 * 1024
VMEM_LIMIT_LN = 60 * 1024 * 1024

X_ROWS = 2 * SEQ
ROW_CHUNK = 1024
FFN_ROW_CHUNK = 512
LN_SUB = 256

F32 = jnp.float32
BF16 = jnp.bfloat16


def _params(*semantics, vmem=VMEM_LIMIT):
    return pltpu.CompilerParams(dimension_semantics=semantics, vmem_limit_bytes=vmem)


def _single(shape, index_map):
    return pl.BlockSpec(shape, index_map, pipeline_mode=pl.Buffered(1))


def _rope_tables_kernel(pos_ref, invf_ret_ref, invf_mla_ref, x_ref, cos_r, sin_r, c_m, sp_m, sn_m, xb_ref):
    xb_ref[...] = x_ref[...].astype(BF16)
    pos = pos_ref[...].astype(F32)
    ang = pos * invf_ret_ref[...]
    cos_r[...] = jnp.cos(ang)
    sin_r[...] = jnp.sin(ang)
    angm = pos * invf_mla_ref[...]
    lane = lax.broadcasted_iota(jnp.int32, angm.shape, 1)
    half = MLA_ROPE // 2
    c = jnp.cos(angm)
    s = jnp.sin(angm)
    c_m[...] = jnp.where(lane < MLA_ROPE, c, 0.0)
    sp_m[...] = jnp.where((lane >= half) & (lane < MLA_ROPE), s, 0.0)
    sn_m[...] = jnp.where(lane < half, -s, 0.0)


def _rope_tables(positions, x):
    tm = 1024
    pos = positions.reshape(TOKENS, 1)
    invf_ret = ROPE_BASE ** (-jnp.arange(0, RET_QK_DIM, 2, dtype=F32) / RET_QK_DIM)
    invf_mla = ROPE_BASE ** (-jnp.arange(0, MLA_ROPE, 2, dtype=F32) / MLA_ROPE)
    invf_mla = jnp.concatenate([invf_mla, invf_mla, jnp.zeros((LANES - MLA_ROPE,), F32)])
    row = pl.BlockSpec((tm, LANES), lambda i: (i, 0))
    const = pl.BlockSpec((1, LANES), lambda i: (0, 0))
    wide = pl.BlockSpec((tm, D_MODEL), lambda i: (i, 0))
    return pl.pallas_call(
        _rope_tables_kernel,
        grid=(TOKENS // tm,),
        in_specs=[pl.BlockSpec((tm, 1), lambda i: (i, 0)), const, const, wide],
        out_specs=[row] * 5 + [wide],
        out_shape=[jax.ShapeDtypeStruct((TOKENS, LANES), F32)] * 5 + [jax.ShapeDtypeStruct((TOKENS, D_MODEL), BF16)],
        compiler_params=_params("arbitrary"),
        name="rope_tables",
    )(pos, invf_ret.reshape(1, LANES), invf_mla.reshape(1, LANES), x)


def _rope_mla(x, c, sp, sn):
    half = MLA_ROPE // 2
    return x * c + pltpu.roll(x, half, 1) * sp + pltpu.roll(x, LANES - half, 1) * sn


def _row_chunks(rows):
    chunk = min(ROW_CHUNK, rows)
    return [(r, chunk) for r in range(0, rows, chunk)]


def _cast_slab(src_ref, dst_ref):
    dst_ref[...] = src_ref[...].astype(BF16)


def _slab_specs(w_stack, layer, grid):
    k, n = w_stack.shape[1:]
    steps = 1
    for g in grid:
        steps *= g
    rows = k // steps
    assert rows * steps == k and rows % BF16_SUBLANES == 0
    if len(grid) == 1:
        flat = lambda i: i
    else:
        flat = lambda i, j: i * grid[1] + j
    src = pl.BlockSpec((None, rows, n), lambda *ids: (layer, flat(*ids), 0))
    dst = pl.BlockSpec((rows, n), lambda *ids: (flat(*ids), 0))
    return src, dst, jax.ShapeDtypeStruct((k, n), BF16)


def _xres_matmul_kernel(x_ref, w_ref, o_ref):
    w = w_ref[...].astype(BF16)
    for r, n in _row_chunks(x_ref.shape[0]):
        o_ref[r:r + n, :] = jnp.dot(x_ref[r:r + n, :], w, preferred_element_type=F32).astype(o_ref.dtype)


def _xres_matmul_cast_kernel(x_ref, w_ref, slab_ref, o_ref, slab_out_ref):
    _cast_slab(slab_ref, slab_out_ref)
    _xres_matmul_kernel(x_ref, w_ref, o_ref)


def _w_cols(w_stack, layer, col0, tn):
    k = w_stack.shape[1]
    first = col0 // tn
    return pl.BlockSpec((None, k, tn), lambda i, j: (layer, 0, first + j))


def _xres_matmul(x, w_stack, layer, col0, n, tn, name, cast=None):
    m, k = x.shape
    rows = min(X_ROWS, m)
    grid = (m // rows, n // tn)
    in_specs = [_single((rows, k), lambda i, j: (i, 0)), _w_cols(w_stack, layer, col0, tn)]
    out_specs = [pl.BlockSpec((rows, tn), lambda i, j: (i, j))]
    out_shape = [jax.ShapeDtypeStruct((m, n), BF16)]
    args = [x, w_stack]
    body = _xres_matmul_kernel
    if cast is not None:
        src, dst, shape = _slab_specs(cast[0], cast[1], grid)
        in_specs.append(src)
        out_specs.append(dst)
        out_shape.append(shape)
        args.append(cast[0])
        body = _xres_matmul_cast_kernel
    out = pl.pallas_call(
        body,
        grid=grid,
        in_specs=in_specs,
        out_specs=out_specs,
        out_shape=out_shape,
        compiler_params=_params("arbitrary", "arbitrary"),
        name=name,
    )(*args)
    return out if cast is not None else out[0]


def _ret_qk_kernel(tn, x_ref, w_ref, cos_ref, sin_ref, o_ref):
    j = pl.program_id(1)
    w = w_ref[...].astype(BF16)
    scale = jnp.where(j >= RET_QK_WIDTH // tn, RET_QK_DIM ** -0.5, 1.0).astype(F32)
    half = RET_QK_DIM // 2
    for r, n in _row_chunks(x_ref.shape[0]):
        acc = jnp.dot(x_ref[r:r + n, :], w, preferred_element_type=F32)
        cos = cos_ref[r:r + n, :]
        sin = sin_ref[r:r + n, :]
        for lo in range(0, tn, RET_QK_DIM):
            x1 = acc[:, lo:lo + half]
            x2 = acc[:, lo + half:lo + RET_QK_DIM]
            o_ref[r:r + n, lo:lo + half] = ((x1 * cos - x2 * sin) * scale).astype(o_ref.dtype)
            o_ref[r:r + n, lo + half:lo + RET_QK_DIM] = ((x2 * cos + x1 * sin) * scale).astype(o_ref.dtype)


def _ret_qk(xb, w_stack, layer, cos, sin):
    tn = 512
    n = 2 * RET_QK_WIDTH
    rows = X_ROWS
    return pl.pallas_call(
        functools.partial(_ret_qk_kernel, tn),
        grid=(TOKENS // rows, n // tn),
        in_specs=[_single((rows, D_MODEL), lambda i, j: (i, 0)),
                  _w_cols(w_stack, layer, 0, tn),
                  _single((rows, LANES), lambda i, j: (i, 0)),
                  _single((rows, LANES), lambda i, j: (i, 0))],
        out_specs=pl.BlockSpec((rows, tn), lambda i, j: (i, j)),
        out_shape=jax.ShapeDtypeStruct((TOKENS, n), BF16),
        compiler_params=_params("arbitrary", "arbitrary"),
        name="ret_qk",
    )(xb, w_stack, cos, sin)


def _ffn_up_kernel(x_ref, wg_ref, wv_ref, cwg_ref, cwv_ref, cbg_ref, cbv_ref, slab_ref,
                   o_ref, slab_out_ref, hg_ref, hv_ref):
    _cast_slab(slab_ref, slab_out_ref)
    halo = BF16_SUBLANES
    wg = wg_ref[...].astype(BF16)
    wv = wv_ref[...].astype(BF16)
    for r in range(0, x_ref.shape[0], FFN_ROW_CHUNK):
        n = FFN_ROW_CHUNK
        seq_start = r % SEQ == 0

        def conv(w, cw_ref, cb_ref, h_ref):
            if seq_start:
                h_ref[0:halo, :] = jnp.zeros((halo, h_ref.shape[1]), F32)
                h_ref[halo:, :] = jnp.dot(x_ref[r:r + n, :], w, preferred_element_type=F32)
            else:
                h_ref[...] = jnp.dot(x_ref[r - halo:r + n, :], w, preferred_element_type=F32)
            cw = cw_ref[...]
            out = (cw[2:3] * h_ref[halo:halo + n, :] + cw[1:2] * h_ref[halo - 1:halo - 1 + n, :]
                   + cw[0:1] * h_ref[halo - 2:halo - 2 + n, :])
            return out + cb_ref[...]

        gate = conv(wg, cwg_ref, cbg_ref, hg_ref)
        val = conv(wv, cwv_ref, cbv_ref, hv_ref)
        o_ref[r:r + n, :] = (gate * jax.nn.sigmoid(gate) * val).astype(o_ref.dtype)


def _ffn_up(xb, w_up, conv_w, conv_b, w_down, layer):
    tn = 512
    nj = D_FF // tn
    rows = X_ROWS
    grid = (TOKENS // rows, nj)
    conv_b = conv_b.reshape(DEPTH, 1, 2 * D_FF)
    taps = lambda depth, first: pl.BlockSpec((None, depth, tn), lambda i, j: (layer, 0, first + j))
    slab_src, slab_dst, slab_shape = _slab_specs(w_down, layer, grid)
    return pl.pallas_call(
        _ffn_up_kernel,
        grid=grid,
        in_specs=[_single((rows, D_MODEL), lambda i, j: (i, 0)),
                  _w_cols(w_up, layer, 0, tn), _w_cols(w_up, layer, D_FF, tn),
                  taps(CONV_W, 0), taps(CONV_W, nj), taps(1, 0), taps(1, nj), slab_src],
        out_specs=[pl.BlockSpec((rows, tn), lambda i, j: (i, j)), slab_dst],
        out_shape=[jax.ShapeDtypeStruct((TOKENS, D_FF), BF16), slab_shape],
        scratch_shapes=[pltpu.VMEM((BF16_SUBLANES + FFN_ROW_CHUNK, tn), F32)] * 2,
        compiler_params=_params("arbitrary", "arbitrary"),
        name="ffn_up",
    )(xb, w_up, w_up, conv_w, conv_w, conv_b, conv_b, w_down)


def _mm_res_ln_kernel(a_ref, w_ref, h_ref, g_ref, b_ref, of_ref, ob_ref):
    g = g_ref[...]
    b = b_ref[...]
    for r in range(0, a_ref.shape[0], LN_SUB):
        rows = slice(r, r + LN_SUB)
        y = DEEPNORM_ALPHA * h_ref[rows, :] + jnp.dot(a_ref[rows, :], w_ref[...], preferred_element_type=F32)
        mu = jnp.mean(y, -1, keepdims=True)
        d = y - mu
        var = jnp.mean(d * d, -1, keepdims=True)
        out = d * lax.rsqrt(var + LN_EPS) * g + b
        of_ref[rows, :] = out
        ob_ref[rows, :] = out.astype(BF16)


def _mm_res_ln(a, w, h, g, b, name):
    m, k = a.shape
    n = w.shape[1]
    tm = 1024 if k <= D_MODEL else 512
    row = lambda width: pl.BlockSpec((tm, width), lambda i: (i, 0))
    vec = pl.BlockSpec((1, n), lambda i: (0, 0))
    return pl.pallas_call(
        _mm_res_ln_kernel,
        grid=(m // tm,),
        in_specs=[row(k), _single((k, n), lambda i: (0, 0)), row(n), vec, vec],
        out_specs=[row(n), row(n)],
        out_shape=[jax.ShapeDtypeStruct((m, n), F32), jax.ShapeDtypeStruct((m, n), BF16)],
        compiler_params=_params("arbitrary", vmem=VMEM_LIMIT_LN),
        name=name,
    )(a, w, h, g.reshape(1, n), b.reshape(1, n))


def _ret_core_kernel(lg_ref, q_ref, k_ref, v_ref, g_ref, gn_ref, y_ref, state_ref, decay_ref):
    L = RET_BLOCK
    head = pl.program_id(1)
    step = pl.program_id(2)
    lg = lg_ref[head]

    @pl.when(step == 0)
    def _():
        state_ref[...] = jnp.zeros_like(state_ref)
        n = lax.broadcasted_iota(jnp.int32, (L, L), 0)
        m = lax.broadcasted_iota(jnp.int32, (L, L), 1)
        dist = jnp.abs(n - m).astype(F32)
        decay_ref[...] = jnp.where((m // CHUNK) <= (n // CHUNK), jnp.exp(lg * dist), 0.0)

    idx = lax.broadcasted_iota(jnp.int32, (L, 1), 0).astype(F32)
    q_decay = jnp.exp(idx * lg)
    k_decay = jnp.exp((L - idx) * lg)
    block_decay = jnp.exp(jnp.full((1, RET_V_DIM), L, F32) * lg)
    decay = decay_ref[...]
    gn = gn_ref[...]
    state = state_ref[...]
    for r in range(0, q_ref.shape[0], L):
        rows = slice(r, r + L)
        q = q_ref[rows, :]
        k = k_ref[rows, :]
        v = v_ref[rows, :]
        s = lax.dot_general(q, k, (((1,), (1,)), ((), ())), preferred_element_type=F32)
        o = jnp.dot((s * decay).astype(BF16), v, preferred_element_type=F32)
        o = o + jnp.dot(q, state.astype(BF16), preferred_element_type=F32) * q_decay
        k_dec = (k.astype(F32) * k_decay).astype(BF16)
        upd = lax.dot_general(k_dec, v, (((0,), (0,)), ((), ())), preferred_element_type=F32)
        state = state * block_decay + upd

        mu = jnp.mean(o, -1, keepdims=True)
        d = o - mu
        var = jnp.mean(d * d, -1, keepdims=True)
        normed = d * lax.rsqrt(var + LN_EPS) * gn
        gate = g_ref[rows, :].astype(F32)
        y_ref[rows, :] = (gate * jax.nn.sigmoid(gate) * normed).astype(y_ref.dtype)
    state_ref[...] = state


def _ret_core(qk, vg, gn_g):
    ts = SEQ
    ns = SEQ // ts
    log_gamma = jnp.log(1.0 - 2.0 ** (-5.0 - jnp.arange(RET_HEADS, dtype=F32)))
    return pl.pallas_call(
        _ret_core_kernel,
        grid=(BATCH, RET_HEADS, ns),
        in_specs=[pl.BlockSpec(memory_space=pltpu.SMEM),
                  pl.BlockSpec((ts, RET_QK_DIM), lambda b, h, c: (b * ns + c, h)),
                  pl.BlockSpec((ts, RET_QK_DIM), lambda b, h, c: (b * ns + c, RET_HEADS + h)),
                  pl.BlockSpec((ts, RET_V_DIM), lambda b, h, c: (b * ns + c, h)),
                  pl.BlockSpec((ts, RET_V_DIM), lambda b, h, c: (b * ns + c, RET_HEADS + h)),
                  pl.BlockSpec((1, RET_V_DIM), lambda b, h, c: (0, h))],
        out_specs=pl.BlockSpec((ts, RET_V_DIM), lambda b, h, c: (b * ns + c, h)),
        out_shape=jax.ShapeDtypeStruct((TOKENS, RET_V_WIDTH), BF16),
        scratch_shapes=[pltpu.VMEM((RET_QK_DIM, RET_V_DIM), F32),
                        pltpu.VMEM((RET_BLOCK, RET_BLOCK), F32)],
        compiler_params=_params("arbitrary", "arbitrary", "arbitrary"),
        name="ret_core",
    )(log_gamma, qk, qk, vg, vg, gn_g.reshape(1, RET_V_WIDTH))


def _rms(x, g):
    return x * lax.rsqrt(jnp.mean(x * x, -1, keepdims=True) + RMS_EPS) * g


def _mla_proj_kernel(x_ref, w_in_ref, qg_ref, kvg_ref, w_uq_ref, w_uk_ref, w_uv_ref,
                     c_ref, sp_ref, sn_ref, slab_ref, q_ref, k_ref, v_ref, slab_out_ref):
    _cast_slab(slab_ref, slab_out_ref)
    c = c_ref[...]
    sp = sp_ref[...]
    sn = sn_ref[...]
    proj = jnp.dot(x_ref[...], w_in_ref[...], preferred_element_type=F32)
    c_q = _rms(proj[:, :MLA_Q_RANK], qg_ref[...]).astype(BF16)
    c_kv = _rms(proj[:, MLA_Q_RANK:MLA_Q_RANK + MLA_KV_RANK], kvg_ref[...]).astype(BF16)
    k_rope = _rope_mla(proj[:, MLA_Q_RANK + MLA_KV_RANK:], c, sp, sn).astype(BF16)

    q = jnp.dot(c_q, w_uq_ref[...], preferred_element_type=F32)
    k_nope = jnp.dot(c_kv, w_uk_ref[...], preferred_element_type=F32)
    for hh in range(MLA_HEADS):
        lo = hh * MLA_QK_PAD
        q_ref[:, lo:lo + MLA_NOPE] = (q[:, lo:lo + MLA_NOPE] * MLA_Q_SCALE).astype(BF16)
        q_ref[:, lo + MLA_NOPE:lo + MLA_QK_PAD] = (_rope_mla(
            q[:, lo + MLA_NOPE:lo + MLA_QK_PAD], c, sp, sn) * MLA_Q_SCALE).astype(BF16)
        k_ref[:, lo:lo + MLA_NOPE] = k_nope[:, hh * MLA_NOPE:(hh + 1) * MLA_NOPE].astype(BF16)
        k_ref[:, lo + MLA_NOPE:lo + MLA_QK_PAD] = k_rope
    v_ref[...] = jnp.dot(c_kv, w_uv_ref[...], preferred_element_type=F32).astype(BF16)


def _mla_proj(xb, w_in, qg, kvg, w_uq, w_uk, w_uv, c, sp, sn, w_out, layer):
    tm = 512
    qk_w = MLA_HEADS * MLA_QK_PAD
    v_w = MLA_HEADS * MLA_V
    grid = (TOKENS // tm,)
    row = lambda w: pl.BlockSpec((tm, w), lambda i: (i, 0))
    res = lambda a: _single(a.shape, lambda i: (0, 0))
    slab_src, slab_dst, slab_shape = _slab_specs(w_out, layer, grid)
    return pl.pallas_call(
        _mla_proj_kernel,
        grid=grid,
        in_specs=[row(D_MODEL), res(w_in), res(qg), res(kvg), res(w_uq), res(w_uk), res(w_uv),
                  row(LANES), row(LANES), row(LANES), slab_src],
        out_specs=[row(qk_w), row(qk_w), row(v_w), slab_dst],
        out_shape=[jax.ShapeDtypeStruct((TOKENS, qk_w), BF16),
                   jax.ShapeDtypeStruct((TOKENS, qk_w), BF16),
                   jax.ShapeDtypeStruct((TOKENS, v_w), BF16), slab_shape],
        compiler_params=_params("arbitrary"),
        name="mla_proj",
    )(xb, w_in, qg, kvg, w_uq, w_uk, w_uv, c, sp, sn, w_out)


def _mla_attn_kernel(tq, q_ref, k_ref, v_ref, o_ref, s_ref):
    dims = (((1,), (1,)), ((), ()))
    row = lax.broadcasted_iota(jnp.int32, (tq, tq), 0)
    col = lax.broadcasted_iota(jnp.int32, (tq, tq), 1)
    visible = (row // CHUNK) >= (col // CHUNK)
    for hh in range(MLA_HEADS_PER_STEP):
        qk = slice(hh * MLA_QK_PAD, (hh + 1) * MLA_QK_PAD)
        vc = slice(hh * MLA_V, (hh + 1) * MLA_V)
        for c in range(SEQ // tq):
            lo = c * tq
            n = lo + tq
            buf = s_ref.at[c % 2]
            buf[:, 0:n] = lax.dot_general(q_ref[lo:n, qk], k_ref[0:n, qk], dims, preferred_element_type=F32)
            buf[:, lo:n] = jnp.where(visible, buf[:, lo:n], NEG_INF)
            s = buf[:, 0:n]
            p = jnp.exp2(s - jnp.max(s, -1, keepdims=True))
            l = jnp.sum(p, -1, keepdims=True)
            acc = jnp.dot(p.astype(BF16), v_ref[0:n, vc], preferred_element_type=F32)
            o_ref[lo:n, vc] = (acc * (1.0 / l)).astype(o_ref.dtype)


def _mla_attn(q, k, v):
    tq = 256
    hp = MLA_HEADS_PER_STEP
    return pl.pallas_call(
        functools.partial(_mla_attn_kernel, tq),
        grid=(BATCH, MLA_HEADS // hp),
        in_specs=[pl.BlockSpec((SEQ, hp * MLA_QK_PAD), lambda b, h: (b, h)),
                  pl.BlockSpec((SEQ, hp * MLA_QK_PAD), lambda b, h: (b, h)),
                  pl.BlockSpec((SEQ, hp * MLA_V), lambda b, h: (b, h))],
        out_specs=pl.BlockSpec((SEQ, hp * MLA_V), lambda b, h: (b, h)),
        out_shape=jax.ShapeDtypeStruct((TOKENS, MLA_HEADS * MLA_V), BF16),
        scratch_shapes=[pltpu.VMEM((2, tq, SEQ), F32)],
        compiler_params=_params("arbitrary", "arbitrary"),
        name="mla_attn",
    )(q, k, v)


def _xa_attn_kernel(q_ref, kv_ref, o_ref):
    for hh in range(XA_HEADS):
        cols = slice(hh * XA_DIM, (hh + 1) * XA_DIM)
        vcols = slice(D_MODEL + hh * XA_DIM, D_MODEL + (hh + 1) * XA_DIM)
        s = lax.dot_general(q_ref[:, cols], kv_ref[:, cols], (((1,), (1,)), ((), ())),
                            preferred_element_type=F32) * (XA_DIM ** -0.5)
        e = jnp.exp(s - jnp.max(s, -1, keepdims=True))
        p = e / jnp.sum(e, -1, keepdims=True)
        o_ref[:, cols] = jnp.dot(p.astype(BF16), kv_ref[:, vcols],
                                 preferred_element_type=F32).astype(o_ref.dtype)


def _xa_attn(q, kv):
    tq = 1024
    nq = SEQ // tq
    return pl.pallas_call(
        _xa_attn_kernel,
        grid=(BATCH, nq),
        in_specs=[pl.BlockSpec((tq, D_MODEL), lambda b, i: (b * nq + i, 0)),
                  pl.BlockSpec((MEM_LEN, 2 * D_MODEL), lambda b, i: (b, 0))],
        out_specs=pl.BlockSpec((tq, D_MODEL), lambda b, i: (b * nq + i, 0)),
        out_shape=jax.ShapeDtypeStruct((TOKENS, D_MODEL), BF16),
        compiler_params=_params("arbitrary", "arbitrary"),
        name="xa_attn",
    )(q, kv)


def _mla_weights(w_in, w_uq, w_ukv):
    pad = LANES - MLA_ROPE
    w_in = jnp.pad(w_in, ((0, 0), (0, pad)))
    w_uq = w_uq.reshape(MLA_Q_RANK, MLA_HEADS, MLA_NOPE + MLA_ROPE)
    w_uq = jnp.pad(w_uq, ((0, 0), (0, 0), (0, pad))).reshape(MLA_Q_RANK, MLA_HEADS * MLA_QK_PAD)
    w_ukv = w_ukv.reshape(MLA_KV_RANK, MLA_HEADS, MLA_NOPE + MLA_V)
    w_uk = w_ukv[:, :, :MLA_NOPE].reshape(MLA_KV_RANK, MLA_HEADS * MLA_NOPE)
    w_uv = w_ukv[:, :, MLA_NOPE:].reshape(MLA_KV_RANK, MLA_HEADS * MLA_V)
    return w_in.astype(BF16), w_uq.astype(BF16), w_uk.astype(BF16), w_uv.astype(BF16)


def kernel(x, mem, positions, ret_w_in, ret_gn_g, ret_w_out, mla_w_in, mla_q_norm_g, mla_w_uq, mla_kv_norm_g, mla_w_ukv, mla_w_out, xa_w_q, xa_w_kv, xa_w_out, ffn_w_up, ffn_conv_w, ffn_conv_b, ffn_w_down, ln_mix_g, ln_mix_b, ln_mem_g, ln_mem_b, ln_ffn_g, ln_ffn_b):
    h = x.reshape(TOKENS, D_MODEL)
    mem_b = mem.reshape(BATCH * MEM_LEN, D_MODEL).astype(BF16)
    cos_r, sin_r, c_m, sp_m, sn_m, hb = _rope_tables(positions, h)

    for layer in range(DEPTH):
        j = layer // N_MIXERS
        if layer % N_MIXERS == 0:
            qk = _ret_qk(hb, ret_w_in, j, cos_r, sin_r)
            vg, w_out = _xres_matmul(hb, ret_w_in, j, 2 * RET_QK_WIDTH, 2 * RET_V_WIDTH, tn=512,
                                     name="ret_vg", cast=(ret_w_out, j))
            mix_in = _ret_core(qk, vg, ret_gn_g[j])
        else:
            w_in, w_uq, w_uk, w_uv = _mla_weights(mla_w_in[j], mla_w_uq[j], mla_w_ukv[j])
            q, k, v, w_out = _mla_proj(hb, w_in, mla_q_norm_g[j].reshape(1, MLA_Q_RANK),
                                       mla_kv_norm_g[j].reshape(1, MLA_KV_RANK), w_uq, w_uk, w_uv,
                                       c_m, sp_m, sn_m, mla_w_out, j)
            mix_in = _mla_attn(q, k, v)
        h, hb = _mm_res_ln(mix_in, w_out, h, ln_mix_g[layer], ln_mix_b[layer], name="mix_out_ln")

        xq, xa_out_w = _xres_matmul(hb, xa_w_q, layer, 0, D_MODEL, tn=512, name="xa_q", cast=(xa_w_out, layer))
        xkv = _xres_matmul(mem_b, xa_w_kv, layer, 0, 2 * D_MODEL, tn=512, name="xa_kv")
        xo = _xa_attn(xq, xkv)
        h, hb = _mm_res_ln(xo, xa_out_w, h, ln_mem_g[layer], ln_mem_b[layer], name="xa_out_ln")

        act, w_down = _ffn_up(hb, ffn_w_up, ffn_conv_w, ffn_conv_b, ffn_w_down, layer)
        h, hb = _mm_res_ln(act, w_down, h, ln_ffn_g[layer], ln_ffn_b[layer], name="ffn_down_ln")

    return h.reshape(BATCH, SEQ, D_MODEL)
```

```python
import functools

import jax
import jax.numpy as jnp
from jax import lax
from jax.experimental import pallas as pl
from jax.experimental.pallas import tpu as pltpu

D_MODEL = 2048
BATCH = 4
SEQ = 2048
DEPTH = 4
CHUNK = 64
MEM_LEN = 256
N_MIXERS = 2

RET_HEADS = 8
RET_QK_DIM = D_MODEL // RET_HEADS
RET_V_DIM = 2 * D_MODEL // RET_HEADS
RET_QK_WIDTH = RET_HEADS * RET_QK_DIM
RET_V_WIDTH = RET_HEADS * RET_V_DIM

MLA_HEADS = 16
MLA_Q_RANK = 512
MLA_KV_RANK = 512
MLA_NOPE = 128
MLA_ROPE = 64
MLA_V = 128

XA_HEADS = 4
XA_DIM = D_MODEL // XA_HEADS

D_FF = 5632
CONV_W = 3

ROPE_BASE = 10000.0
LN_EPS = 1e-5
RMS_EPS = 1e-6
NEG_INF = -1e30
DEEPNORM_ALPHA = (2 * DEPTH) ** 0.25
LOG2_E = 1.4426950408889634
MLA_Q_SCALE = (MLA_NOPE + MLA_ROPE) ** -0.5 * LOG2_E
MLA_HEADS_PER_STEP = 2

TOKENS = BATCH * SEQ
LANES = 128
BF16_SUBLANES = 16
MLA_QK_PAD = 256
RET_BLOCK = 256
VMEM_LIMIT = 56 * 1024 * 1024
VMEM_LIMIT_LN = 60 * 1024 * 1024

X_ROWS = 2 * SEQ
ROW_CHUNK = 1024
FFN_ROW_CHUNK = 512
LN_SUB = 256

F32 = jnp.float32
BF16 = jnp.bfloat16


def _params(*semantics, vmem=VMEM_LIMIT):
    return pltpu.CompilerParams(dimension_semantics=semantics, vmem_limit_bytes=vmem)


def _single(shape, index_map):
    return pl.BlockSpec(shape, index_map, pipeline_mode=pl.Buffered(1))


def _rope_tables_kernel(pos_ref, invf_ret_ref, invf_mla_ref, x_ref, cos_r, sin_r, c_m, sp_m, sn_m, xb_ref):
    xb_ref[...] = x_ref[...].astype(BF16)
    pos = pos_ref[...].astype(F32)
    ang = pos * invf_ret_ref[...]
    cos_r[...] = jnp.cos(ang)
    sin_r[...] = jnp.sin(ang)
    angm = pos * invf_mla_ref[...]
    lane = lax.broadcasted_iota(jnp.int32, angm.shape, 1)
    half = MLA_ROPE // 2
    c = jnp.cos(angm)
    s = jnp.sin(angm)
    c_m[...] = jnp.where(lane < MLA_ROPE, c, 0.0)
    sp_m[...] = jnp.where((lane >= half) & (lane < MLA_ROPE), s, 0.0)
    sn_m[...] = jnp.where(lane < half, -s, 0.0)


def _rope_tables(positions, x):
    tm = 1024
    pos = positions.reshape(TOKENS, 1)
    invf_ret = ROPE_BASE ** (-jnp.arange(0, RET_QK_DIM, 2, dtype=F32) / RET_QK_DIM)
    invf_mla = ROPE_BASE ** (-jnp.arange(0, MLA_ROPE, 2, dtype=F32) / MLA_ROPE)
    invf_mla = jnp.concatenate([invf_mla, invf_mla, jnp.zeros((LANES - MLA_ROPE,), F32)])
    row = pl.BlockSpec((tm, LANES), lambda i: (i, 0))
    const = pl.BlockSpec((1, LANES), lambda i: (0, 0))
    wide = pl.BlockSpec((tm, D_MODEL), lambda i: (i, 0))
    return pl.pallas_call(
        _rope_tables_kernel,
        grid=(TOKENS // tm,),
        in_specs=[pl.BlockSpec((tm, 1), lambda i: (i, 0)), const, const, wide],
        out_specs=[row] * 5 + [wide],
        out_shape=[jax.ShapeDtypeStruct((TOKENS, LANES), F32)] * 5 + [jax.ShapeDtypeStruct((TOKENS, D_MODEL), BF16)],
        compiler_params=_params("arbitrary"),
        name="rope_tables",
    )(pos, invf_ret.reshape(1, LANES), invf_mla.reshape(1, LANES), x)


def _rope_mla(x, c, sp, sn):
    half = MLA_ROPE // 2
    return x * c + pltpu.roll(x, half, 1) * sp + pltpu.roll(x, LANES - half, 1) * sn


def _row_chunks(rows):
    chunk = min(ROW_CHUNK, rows)
    return [(r, chunk) for r in range(0, rows, chunk)]


def _cast_slab(src_ref, dst_ref):
    dst_ref[...] = src_ref[...].astype(BF16)


def _slab_specs(w_stack, layer, grid):
    k, n = w_stack.shape[1:]
    steps = 1
    for g in grid:
        steps *= g
    rows = k // steps
    assert rows * steps == k and rows % BF16_SUBLANES == 0
    if len(grid) == 1:
        flat = lambda i: i
    else:
        flat = lambda i, j: i * grid[1] + j
    src = pl.BlockSpec((None, rows, n), lambda *ids: (layer, flat(*ids), 0))
    dst = pl.BlockSpec((rows, n), lambda *ids: (flat(*ids), 0))
    return src, dst, jax.ShapeDtypeStruct((k, n), BF16)


def _xres_matmul_kernel(x_ref, w_ref, o_ref):
    w = w_ref[...].astype(BF16)
    for r, n in _row_chunks(x_ref.shape[0]):
        o_ref[r:r + n, :] = jnp.dot(x_ref[r:r + n, :], w, preferred_element_type=F32).astype(o_ref.dtype)


def _xres_matmul_cast_kernel(x_ref, w_ref, slab_ref, o_ref, slab_out_ref):
    _cast_slab(slab_ref, slab_out_ref)
    _xres_matmul_kernel(x_ref, w_ref, o_ref)


def _w_cols(w_stack, layer, col0, tn):
    k = w_stack.shape[1]
    first = col0 // tn
    return pl.BlockSpec((None, k, tn), lambda i, j: (layer, 0, first + j))


def _xres_matmul(x, w_stack, layer, col0, n, tn, name, cast=None):
    m, k = x.shape
    rows = min(X_ROWS, m)
    grid = (m // rows, n // tn)
    in_specs = [_single((rows, k), lambda i, j: (i, 0)), _w_cols(w_stack, layer, col0, tn)]
    out_specs = [pl.BlockSpec((rows, tn), lambda i, j: (i, j))]
    out_shape = [jax.ShapeDtypeStruct((m, n), BF16)]
    args = [x, w_stack]
    body = _xres_matmul_kernel
    if cast is not None:
        src, dst, shape = _slab_specs(cast[0], cast[1], grid)
        in_specs.append(src)
        out_specs.append(dst)
        out_shape.append(shape)
        args.append(cast[0])
        body = _xres_matmul_cast_kernel
    out = pl.pallas_call(
        body,
        grid=grid,
        in_specs=in_specs,
        out_specs=out_specs,
        out_shape=out_shape,
        compiler_params=_params("arbitrary", "arbitrary"),
        name=name,
    )(*args)
    return out if cast is not None else out[0]


def _ret_qk_kernel(tn, x_ref, w_ref, cos_ref, sin_ref, o_ref):
    j = pl.program_id(1)
    w = w_ref[...].astype(BF16)
    scale = jnp.where(j >= RET_QK_WIDTH // tn, RET_QK_DIM ** -0.5, 1.0).astype(F32)
    half = RET_QK_DIM // 2
    for r, n in _row_chunks(x_ref.shape[0]):
        acc = jnp.dot(x_ref[r:r + n, :], w, preferred_element_type=F32)
        cos = cos_ref[r:r + n, :]
        sin = sin_ref[r:r + n, :]
        for lo in range(0, tn, RET_QK_DIM):
            x1 = acc[:, lo:lo + half]
            x2 = acc[:, lo + half:lo + RET_QK_DIM]
            o_ref[r:r + n, lo:lo + half] = ((x1 * cos - x2 * sin) * scale).astype(o_ref.dtype)
            o_ref[r:r + n, lo + half:lo + RET_QK_DIM] = ((x2 * cos + x1 * sin) * scale).astype(o_ref.dtype)


def _ret_qk(xb, w_stack, layer, cos, sin):
    tn = 512
    n = 2 * RET_QK_WIDTH
    rows = X_ROWS
    return pl.pallas_call(
        functools.partial(_ret_qk_kernel, tn),
        grid=(TOKENS // rows, n // tn),
        in_specs=[_single((rows, D_MODEL), lambda i, j: (i, 0)),
                  _w_cols(w_stack, layer, 0, tn),
                  _single((rows, LANES), lambda i, j: (i, 0)),
                  _single((rows, LANES), lambda i, j: (i, 0))],
        out_specs=pl.BlockSpec((rows, tn), lambda i, j: (i, j)),
        out_shape=jax.ShapeDtypeStruct((TOKENS, n), BF16),
        compiler_params=_params("arbitrary", "arbitrary"),
        name="ret_qk",
    )(xb, w_stack, cos, sin)


def _ffn_up_kernel(x_ref, wg_ref, wv_ref, cwg_ref, cwv_ref, cbg_ref, cbv_ref, slab_ref,
                   o_ref, slab_out_ref, hg_ref, hv_ref):
    _cast_slab(slab_ref, slab_out_ref)
    halo = BF16_SUBLANES
    wg = wg_ref[...].astype(BF16)
    wv = wv_ref[...].astype(BF16)
    for r in range(0, x_ref.shape[0], FFN_ROW_CHUNK):
        n = FFN_ROW_CHUNK
        seq_start = r % SEQ == 0

        def conv(w, cw_ref, cb_ref, h_ref):
            if seq_start:
                h_ref[0:halo, :] = jnp.zeros((halo, h_ref.shape[1]), F32)
                h_ref[halo:, :] = jnp.dot(x_ref[r:r + n, :], w, preferred_element_type=F32)
            else:
                h_ref[...] = jnp.dot(x_ref[r - halo:r + n, :], w, preferred_element_type=F32)
            cw = cw_ref[...]
            out = (cw[2:3] * h_ref[halo:halo + n, :] + cw[1:2] * h_ref[halo - 1:halo - 1 + n, :]
                   + cw[0:1] * h_ref[halo - 2:halo - 2 + n, :])
            return out + cb_ref[...]

        gate = conv(wg, cwg_ref, cbg_ref, hg_ref)
        val = conv(wv, cwv_ref, cbv_ref, hv_ref)
        o_ref[r:r + n, :] = (gate * jax.nn.sigmoid(gate) * val).astype(o_ref.dtype)


def _ffn_up(xb, w_up, conv_w, conv_b, w_down, layer):
    tn = 512
    nj = D_FF // tn
    rows = X_ROWS
    grid = (TOKENS // rows, nj)
    conv_b = conv_b.reshape(DEPTH, 1, 2 * D_FF)
    taps = lambda depth, first: pl.BlockSpec((None, depth, tn), lambda i, j: (layer, 0, first + j))
    slab_src, slab_dst, slab_shape = _slab_specs(w_down, layer, grid)
    return pl.pallas_call(
        _ffn_up_kernel,
        grid=grid,
        in_specs=[_single((rows, D_MODEL), lambda i, j: (i, 0)),
                  _w_cols(w_up, layer, 0, tn), _w_cols(w_up, layer, D_FF, tn),
                  taps(CONV_W, 0), taps(CONV_W, nj), taps(1, 0), taps(1, nj), slab_src],
        out_specs=[pl.BlockSpec((rows, tn), lambda i, j: (i, j)), slab_dst],
        out_shape=[jax.ShapeDtypeStruct((TOKENS, D_FF), BF16), slab_shape],
        scratch_shapes=[pltpu.VMEM((BF16_SUBLANES + FFN_ROW_CHUNK, tn), F32)] * 2,
        compiler_params=_params("arbitrary", "arbitrary"),
        name="ffn_up",
    )(xb, w_up, w_up, conv_w, conv_w, conv_b, conv_b, w_down)


def _mm_res_ln_kernel(a_ref, w_ref, h_ref, g_ref, b_ref, of_ref, ob_ref):
    g = g_ref[...]
    b = b_ref[...]
    for r in range(0, a_ref.shape[0], LN_SUB):
        rows = slice(r, r + LN_SUB)
        y = DEEPNORM_ALPHA * h_ref[rows, :] + jnp.dot(a_ref[rows, :], w_ref[...], preferred_element_type=F32)
        mu = jnp.mean(y, -1, keepdims=True)
        d = y - mu
        var = jnp.mean(d * d, -1, keepdims=True)
        out = d * lax.rsqrt(var + LN_EPS) * g + b
        of_ref[rows, :] = out
        ob_ref[rows, :] = out.astype(BF16)


def _mm_res_ln(a, w, h, g, b, name):
    m, k = a.shape
    n = w.shape[1]
    tm = 512
    row = lambda width: pl.BlockSpec((tm, width), lambda i: (i, 0))
    vec = pl.BlockSpec((1, n), lambda i: (0, 0))
    return pl.pallas_call(
        _mm_res_ln_kernel,
        grid=(m // tm,),
        in_specs=[row(k), _single((k, n), lambda i: (0, 0)), row(n), vec, vec],
        out_specs=[row(n), row(n)],
        out_shape=[jax.ShapeDtypeStruct((m, n), F32), jax.ShapeDtypeStruct((m, n), BF16)],
        compiler_params=_params("arbitrary", vmem=VMEM_LIMIT_LN if k > RET_V_WIDTH else VMEM_LIMIT),
        name=name,
    )(a, w, h, g.reshape(1, n), b.reshape(1, n))


def _ret_core_kernel(lg_ref, q_ref, k_ref, v_ref, g_ref, gn_ref, y_ref, state_ref, decay_ref):
    L = RET_BLOCK
    head = pl.program_id(1)
    step = pl.program_id(2)
    lg = lg_ref[head]

    @pl.when(step == 0)
    def _():
        state_ref[...] = jnp.zeros_like(state_ref)
        n = lax.broadcasted_iota(jnp.int32, (L, L), 0)
        m = lax.broadcasted_iota(jnp.int32, (L, L), 1)
        dist = jnp.abs(n - m).astype(F32)
        decay_ref[...] = jnp.where((m // CHUNK) <= (n // CHUNK), jnp.exp(lg * dist), 0.0)

    idx = lax.broadcasted_iota(jnp.int32, (L, 1), 0).astype(F32)
    q_decay = jnp.exp(idx * lg)
    k_decay = jnp.exp((L - idx) * lg)
    block_decay = jnp.exp(jnp.full((1, RET_V_DIM), L, F32) * lg)
    decay = decay_ref[...]
    gn = gn_ref[...]
    state = state_ref[...]
    for r in range(0, q_ref.shape[0], L):
        rows = slice(r, r + L)
        q = q_ref[rows, :]
        k = k_ref[rows, :]
        v = v_ref[rows, :]
        s = lax.dot_general(q, k, (((1,), (1,)), ((), ())), preferred_element_type=F32)
        o = jnp.dot((s * decay).astype(BF16), v, preferred_element_type=F32)
        o = o + jnp.dot(q, state.astype(BF16), preferred_element_type=F32) * q_decay
        k_dec = (k.astype(F32) * k_decay).astype(BF16)
        upd = lax.dot_general(k_dec, v, (((0,), (0,)), ((), ())), preferred_element_type=F32)
        state = state * block_decay + upd

        mu = jnp.mean(o, -1, keepdims=True)
        d = o - mu
        var = jnp.mean(d * d, -1, keepdims=True)
        normed = d * lax.rsqrt(var + LN_EPS) * gn
        gate = g_ref[rows, :].astype(F32)
        y_ref[rows, :] = (gate * jax.nn.sigmoid(gate) * normed).astype(y_ref.dtype)
    state_ref[...] = state


def _ret_core(qk, vg, gn_g):
    ts = SEQ
    ns = SEQ // ts
    log_gamma = jnp.log(1.0 - 2.0 ** (-5.0 - jnp.arange(RET_HEADS, dtype=F32)))
    return pl.pallas_call(
        _ret_core_kernel,
        grid=(BATCH, RET_HEADS, ns),
        in_specs=[pl.BlockSpec(memory_space=pltpu.SMEM),
                  pl.BlockSpec((ts, RET_QK_DIM), lambda b, h, c: (b * ns + c, h)),
                  pl.BlockSpec((ts, RET_QK_DIM), lambda b, h, c: (b * ns + c, RET_HEADS + h)),
                  pl.BlockSpec((ts, RET_V_DIM), lambda b, h, c: (b * ns + c, h)),
                  pl.BlockSpec((ts, RET_V_DIM), lambda b, h, c: (b * ns + c, RET_HEADS + h)),
                  pl.BlockSpec((1, RET_V_DIM), lambda b, h, c: (0, h))],
        out_specs=pl.BlockSpec((ts, RET_V_DIM), lambda b, h, c: (b * ns + c, h)),
        out_shape=jax.ShapeDtypeStruct((TOKENS, RET_V_WIDTH), BF16),
        scratch_shapes=[pltpu.VMEM((RET_QK_DIM, RET_V_DIM), F32),
                        pltpu.VMEM((RET_BLOCK, RET_BLOCK), F32)],
        compiler_params=_params("arbitrary", "arbitrary", "arbitrary"),
        name="ret_core",
    )(log_gamma, qk, qk, vg, vg, gn_g.reshape(1, RET_V_WIDTH))


def _rms(x, g):
    return x * lax.rsqrt(jnp.mean(x * x, -1, keepdims=True) + RMS_EPS) * g


def _mla_proj_kernel(x_ref, w_in_ref, qg_ref, kvg_ref, w_uq_ref, w_uk_ref, w_uv_ref,
                     c_ref, sp_ref, sn_ref, slab_ref, q_ref, k_ref, v_ref, slab_out_ref):
    _cast_slab(slab_ref, slab_out_ref)
    c = c_ref[...]
    sp = sp_ref[...]
    sn = sn_ref[...]
    proj = jnp.dot(x_ref[...], w_in_ref[...], preferred_element_type=F32)
    c_q = _rms(proj[:, :MLA_Q_RANK], qg_ref[...]).astype(BF16)
    c_kv = _rms(proj[:, MLA_Q_RANK:MLA_Q_RANK + MLA_KV_RANK], kvg_ref[...]).astype(BF16)
    k_rope = _rope_mla(proj[:, MLA_Q_RANK + MLA_KV_RANK:], c, sp, sn).astype(BF16)

    q = jnp.dot(c_q, w_uq_ref[...], preferred_element_type=F32)
    k_nope = jnp.dot(c_kv, w_uk_ref[...], preferred_element_type=F32)
    for hh in range(MLA_HEADS):
        lo = hh * MLA_QK_PAD
        q_ref[:, lo:lo + MLA_NOPE] = (q[:, lo:lo + MLA_NOPE] * MLA_Q_SCALE).astype(BF16)
        q_ref[:, lo + MLA_NOPE:lo + MLA_QK_PAD] = (_rope_mla(
            q[:, lo + MLA_NOPE:lo + MLA_QK_PAD], c, sp, sn) * MLA_Q_SCALE).astype(BF16)
        k_ref[:, lo:lo + MLA_NOPE] = k_nope[:, hh * MLA_NOPE:(hh + 1) * MLA_NOPE].astype(BF16)
        k_ref[:, lo + MLA_NOPE:lo + MLA_QK_PAD] = k_rope
    v_ref[...] = jnp.dot(c_kv, w_uv_ref[...], preferred_element_type=F32).astype(BF16)


def _mla_proj(xb, w_in, qg, kvg, w_uq, w_uk, w_uv, c, sp, sn, w_out, layer):
    tm = 512
    qk_w = MLA_HEADS * MLA_QK_PAD
    v_w = MLA_HEADS * MLA_V
    grid = (TOKENS // tm,)
    row = lambda w: pl.BlockSpec((tm, w), lambda i: (i, 0))
    res = lambda a: _single(a.shape, lambda i: (0, 0))
    slab_src, slab_dst, slab_shape = _slab_specs(w_out, layer, grid)
    return pl.pallas_call(
        _mla_proj_kernel,
        grid=grid,
        in_specs=[row(D_MODEL), res(w_in), res(qg), res(kvg), res(w_uq), res(w_uk), res(w_uv),
                  row(LANES), row(LANES), row(LANES), slab_src],
        out_specs=[row(qk_w), row(qk_w), row(v_w), slab_dst],
        out_shape=[jax.ShapeDtypeStruct((TOKENS, qk_w), BF16),
                   jax.ShapeDtypeStruct((TOKENS, qk_w), BF16),
                   jax.ShapeDtypeStruct((TOKENS, v_w), BF16), slab_shape],
        compiler_params=_params("arbitrary"),
        name="mla_proj",
    )(xb, w_in, qg, kvg, w_uq, w_uk, w_uv, c, sp, sn, w_out)


def _mla_attn_kernel(tq, q_ref, k_ref, v_ref, o_ref, s_ref):
    dims = (((1,), (1,)), ((), ()))
    row = lax.broadcasted_iota(jnp.int32, (tq, tq), 0)
    col = lax.broadcasted_iota(jnp.int32, (tq, tq), 1)
    visible = (row // CHUNK) >= (col // CHUNK)
    for c in range(SEQ // tq):
        lo = c * tq
        n = lo + tq
        for hh in range(MLA_HEADS_PER_STEP):
            qk = slice(hh * MLA_QK_PAD, (hh + 1) * MLA_QK_PAD)
            vc = slice(hh * MLA_V, (hh + 1) * MLA_V)
            buf = s_ref.at[MLA_HEADS_PER_STEP * (c % 2) + hh]
            buf[:, 0:n] = lax.dot_general(q_ref[lo:n, qk], k_ref[0:n, qk], dims, preferred_element_type=F32)
            buf[:, lo:n] = jnp.where(visible, buf[:, lo:n], NEG_INF)
            s = buf[:, 0:n]
            p = jnp.exp2(s - jnp.max(s, -1, keepdims=True))
            l = jnp.sum(p, -1, keepdims=True)
            acc = jnp.dot(p.astype(BF16), v_ref[0:n, vc], preferred_element_type=F32)
            o_ref[lo:n, vc] = (acc * (1.0 / l)).astype(o_ref.dtype)


def _mla_attn(q, k, v):
    tq = 256
    hp = MLA_HEADS_PER_STEP
    return pl.pallas_call(
        functools.partial(_mla_attn_kernel, tq),
        grid=(BATCH, MLA_HEADS // hp),
        in_specs=[pl.BlockSpec((SEQ, hp * MLA_QK_PAD), lambda b, h: (b, h)),
                  pl.BlockSpec((SEQ, hp * MLA_QK_PAD), lambda b, h: (b, h)),
                  pl.BlockSpec((SEQ, hp * MLA_V), lambda b, h: (b, h))],
        out_specs=pl.BlockSpec((SEQ, hp * MLA_V), lambda b, h: (b, h)),
        out_shape=jax.ShapeDtypeStruct((TOKENS, MLA_HEADS * MLA_V), BF16),
        scratch_shapes=[pltpu.VMEM((2 * hp, tq, SEQ), F32)],
        compiler_params=_params("arbitrary", "arbitrary"),
        name="mla_attn",
    )(q, k, v)


def _xa_attn_kernel(q_ref, kv_ref, o_ref):
    for hh in range(XA_HEADS):
        cols = slice(hh * XA_DIM, (hh + 1) * XA_DIM)
        vcols = slice(D_MODEL + hh * XA_DIM, D_MODEL + (hh + 1) * XA_DIM)
        s = lax.dot_general(q_ref[:, cols], kv_ref[:, cols], (((1,), (1,)), ((), ())),
                            preferred_element_type=F32) * (XA_DIM ** -0.5)
        e = jnp.exp(s - jnp.max(s, -1, keepdims=True))
        p = e / jnp.sum(e, -1, keepdims=True)
        o_ref[:, cols] = jnp.dot(p.astype(BF16), kv_ref[:, vcols],
                                 preferred_element_type=F32).astype(o_ref.dtype)


def _xa_attn(q, kv):
    tq = 1024
    nq = SEQ // tq
    return pl.pallas_call(
        _xa_attn_kernel,
        grid=(BATCH, nq),
        in_specs=[pl.BlockSpec((tq, D_MODEL), lambda b, i: (b * nq + i, 0)),
                  pl.BlockSpec((MEM_LEN, 2 * D_MODEL), lambda b, i: (b, 0))],
        out_specs=pl.BlockSpec((tq, D_MODEL), lambda b, i: (b * nq + i, 0)),
        out_shape=jax.ShapeDtypeStruct((TOKENS, D_MODEL), BF16),
        compiler_params=_params("arbitrary", "arbitrary"),
        name="xa_attn",
    )(q, kv)


def _mla_weights(w_in, w_uq, w_ukv):
    pad = LANES - MLA_ROPE
    w_in = jnp.pad(w_in, ((0, 0), (0, pad)))
    w_uq = w_uq.reshape(MLA_Q_RANK, MLA_HEADS, MLA_NOPE + MLA_ROPE)
    w_uq = jnp.pad(w_uq, ((0, 0), (0, 0), (0, pad))).reshape(MLA_Q_RANK, MLA_HEADS * MLA_QK_PAD)
    w_ukv = w_ukv.reshape(MLA_KV_RANK, MLA_HEADS, MLA_NOPE + MLA_V)
    w_uk = w_ukv[:, :, :MLA_NOPE].reshape(MLA_KV_RANK, MLA_HEADS * MLA_NOPE)
    w_uv = w_ukv[:, :, MLA_NOPE:].reshape(MLA_KV_RANK, MLA_HEADS * MLA_V)
    return w_in.astype(BF16), w_uq.astype(BF16), w_uk.astype(BF16), w_uv.astype(BF16)


def kernel(x, mem, positions, ret_w_in, ret_gn_g, ret_w_out, mla_w_in, mla_q_norm_g, mla_w_uq, mla_kv_norm_g, mla_w_ukv, mla_w_out, xa_w_q, xa_w_kv, xa_w_out, ffn_w_up, ffn_conv_w, ffn_conv_b, ffn_w_down, ln_mix_g, ln_mix_b, ln_mem_g, ln_mem_b, ln_ffn_g, ln_ffn_b):
    h = x.reshape(TOKENS, D_MODEL)
    mem_b = mem.reshape(BATCH * MEM_LEN, D_MODEL).astype(BF16)
    cos_r, sin_r, c_m, sp_m, sn_m, hb = _rope_tables(positions, h)

    for layer in range(DEPTH):
        j = layer // N_MIXERS
        if layer % N_MIXERS == 0:
            qk = _ret_qk(hb, ret_w_in, j, cos_r, sin_r)
            vg, w_out = _xres_matmul(hb, ret_w_in, j, 2 * RET_QK_WIDTH, 2 * RET_V_WIDTH, tn=512,
                                     name="ret_vg", cast=(ret_w_out, j))
            mix_in = _ret_core(qk, vg, ret_gn_g[j])
        else:
            w_in, w_uq, w_uk, w_uv = _mla_weights(mla_w_in[j], mla_w_uq[j], mla_w_ukv[j])
            q, k, v, w_out = _mla_proj(hb, w_in, mla_q_norm_g[j].reshape(1, MLA_Q_RANK),
                                       mla_kv_norm_g[j].reshape(1, MLA_KV_RANK), w_uq, w_uk, w_uv,
                                       c_m, sp_m, sn_m, mla_w_out, j)
            mix_in = _mla_attn(q, k, v)
        h, hb = _mm_res_ln(mix_in, w_out, h, ln_mix_g[layer], ln_mix_b[layer], name="mix_out_ln")

        xq, xa_out_w = _xres_matmul(hb, xa_w_q, layer, 0, D_MODEL, tn=512, name="xa_q", cast=(xa_w_out, layer))
        xkv = _xres_matmul(mem_b, xa_w_kv, layer, 0, 2 * D_MODEL, tn=512, name="xa_kv")
        xo = _xa_attn(xq, xkv)
        h, hb = _mm_res_ln(xo, xa_out_w, h, ln_mem_g[layer], ln_mem_b[layer], name="xa_out_ln")

        act, w_down = _ffn_up(hb, ffn_w_up, ffn_conv_w, ffn_conv_b, ffn_w_down, layer)
        h, hb = _mm_res_ln(act, w_down, h, ln_ffn_g[layer], ln_ffn_b[layer], name="ffn_down_ln")

    return h.reshape(BATCH, SEQ, D_MODEL)
```

```python
import functools

import jax
import jax.numpy as jnp
from jax import lax
from jax.experimental import pallas as pl
from jax.experimental.pallas import tpu as pltpu

D_MODEL = 2048
BATCH = 4
SEQ = 2048
DEPTH = 4
CHUNK = 64
MEM_LEN = 256
N_MIXERS = 2

RET_HEADS = 8
RET_QK_DIM = D_MODEL // RET_HEADS
RET_V_DIM = 2 * D_MODEL // RET_HEADS
RET_QK_WIDTH = RET_HEADS * RET_QK_DIM
RET_V_WIDTH = RET_HEADS * RET_V_DIM

MLA_HEADS = 16
MLA_Q_RANK = 512
MLA_KV_RANK = 512
MLA_NOPE = 128
MLA_ROPE = 64
MLA_V = 128

XA_HEADS = 4
XA_DIM = D_MODEL // XA_HEADS

D_FF = 5632
CONV_W = 3

ROPE_BASE = 10000.0
LN_EPS = 1e-5
RMS_EPS = 1e-6
NEG_INF = -1e30
DEEPNORM_ALPHA = (2 * DEPTH) ** 0.25
LOG2_E = 1.4426950408889634
MLA_Q_SCALE = (MLA_NOPE + MLA_ROPE) ** -0.5 * LOG2_E
MLA_HEADS_PER_STEP = 2

TOKENS = BATCH * SEQ
LANES = 128
BF16_SUBLANES = 16
MLA_QK_PAD = 256
RET_BLOCK = 256
V7X_VMEM_BYTES = 64 * 1024 * 1024
VMEM_LIMIT = V7X_VMEM_BYTES * 7 // 8
VMEM_LIMIT_LN = V7X_VMEM_BYTES * 15 // 16

X_ROWS = 2 * SEQ
ROW_CHUNK = 1024
FFN_ROW_CHUNK = 512
LN_SUB = 256

F32 = jnp.float32
BF16 = jnp.bfloat16


def _params(*semantics, vmem=VMEM_LIMIT):
    return pltpu.CompilerParams(dimension_semantics=semantics, vmem_limit_bytes=vmem)


def _single(shape, index_map):
    return pl.BlockSpec(shape, index_map, pipeline_mode=pl.Buffered(1))


def _rope_tables_kernel(pos_ref, invf_ret_ref, invf_mla_ref, x_ref, cos_r, sin_r, c_m, sp_m, sn_m, xb_ref):
    xb_ref[...] = x_ref[...].astype(BF16)
    pos = pos_ref[...].astype(F32)
    ang = pos * invf_ret_ref[...]
    cos_r[...] = jnp.cos(ang)
    sin_r[...] = jnp.sin(ang)
    angm = pos * invf_mla_ref[...]
    lane = lax.broadcasted_iota(jnp.int32, angm.shape, 1)
    half = MLA_ROPE // 2
    c = jnp.cos(angm)
    s = jnp.sin(angm)
    c_m[...] = jnp.where(lane < MLA_ROPE, c, 0.0)
    sp_m[...] = jnp.where((lane >= half) & (lane < MLA_ROPE), s, 0.0)
    sn_m[...] = jnp.where(lane < half, -s, 0.0)


def _rope_tables(positions, x):
    tm = 1024
    pos = positions.reshape(TOKENS, 1)
    invf_ret = ROPE_BASE ** (-jnp.arange(0, RET_QK_DIM, 2, dtype=F32) / RET_QK_DIM)
    invf_mla = ROPE_BASE ** (-jnp.arange(0, MLA_ROPE, 2, dtype=F32) / MLA_ROPE)
    invf_mla = jnp.concatenate([invf_mla, invf_mla, jnp.zeros((LANES - MLA_ROPE,), F32)])
    row = pl.BlockSpec((tm, LANES), lambda i: (i, 0))
    const = pl.BlockSpec((1, LANES), lambda i: (0, 0))
    wide = pl.BlockSpec((tm, D_MODEL), lambda i: (i, 0))
    return pl.pallas_call(
        _rope_tables_kernel,
        grid=(TOKENS // tm,),
        in_specs=[pl.BlockSpec((tm, 1), lambda i: (i, 0)), const, const, wide],
        out_specs=[row] * 5 + [wide],
        out_shape=[jax.ShapeDtypeStruct((TOKENS, LANES), F32)] * 5 + [jax.ShapeDtypeStruct((TOKENS, D_MODEL), BF16)],
        compiler_params=_params("arbitrary"),
        name="rope_tables",
    )(pos, invf_ret.reshape(1, LANES), invf_mla.reshape(1, LANES), x)


def _rope_mla(x, c, sp, sn):
    half = MLA_ROPE // 2
    return x * c + pltpu.roll(x, half, 1) * sp + pltpu.roll(x, LANES - half, 1) * sn


def _row_chunks(rows):
    chunk = min(ROW_CHUNK, rows)
    return [(r, chunk) for r in range(0, rows, chunk)]


def _cast_slab(src_ref, dst_ref):
    dst_ref[...] = src_ref[...].astype(BF16)


def _slab_specs(w_stack, layer, grid):
    k, n = w_stack.shape[1:]
    steps = 1
    for g in grid:
        steps *= g
    rows = k // steps
    assert rows * steps == k and rows % BF16_SUBLANES == 0
    if len(grid) == 1:
        flat = lambda i: i
    else:
        flat = lambda i, j: i * grid[1] + j
    src = pl.BlockSpec((None, rows, n), lambda *ids: (layer, flat(*ids), 0))
    dst = pl.BlockSpec((rows, n), lambda *ids: (flat(*ids), 0))
    return src, dst, jax.ShapeDtypeStruct((k, n), BF16)


def _xres_matmul_kernel(x_ref, w_ref, o_ref):
    w = w_ref[...].astype(BF16)
    for r, n in _row_chunks(x_ref.shape[0]):
        o_ref[r:r + n, :] = jnp.dot(x_ref[r:r + n, :], w, preferred_element_type=F32).astype(o_ref.dtype)


def _xres_matmul_cast_kernel(x_ref, w_ref, slab_ref, o_ref, slab_out_ref):
    _cast_slab(slab_ref, slab_out_ref)
    _xres_matmul_kernel(x_ref, w_ref, o_ref)


def _w_cols(w_stack, layer, col0, tn):
    k = w_stack.shape[1]
    first = col0 // tn
    return pl.BlockSpec((None, k, tn), lambda i, j: (layer, 0, first + j))


def _xres_matmul(x, w_stack, layer, col0, n, tn, name, cast=None):
    m, k = x.shape
    rows = min(X_ROWS, m)
    grid = (m // rows, n // tn)
    in_specs = [_single((rows, k), lambda i, j: (i, 0)), _w_cols(w_stack, layer, col0, tn)]
    out_specs = [pl.BlockSpec((rows, tn), lambda i, j: (i, j))]
    out_shape = [jax.ShapeDtypeStruct((m, n), BF16)]
    args = [x, w_stack]
    body = _xres_matmul_kernel
    if cast is not None:
        src, dst, shape = _slab_specs(cast[0], cast[1], grid)
        in_specs.append(src)
        out_specs.append(dst)
        out_shape.append(shape)
        args.append(cast[0])
        body = _xres_matmul_cast_kernel
    out = pl.pallas_call(
        body,
        grid=grid,
        in_specs=in_specs,
        out_specs=out_specs,
        out_shape=out_shape,
        compiler_params=_params("arbitrary", "arbitrary"),
        name=name,
    )(*args)
    return out if cast is not None else out[0]


def _ret_qk_kernel(tn, x_ref, w_ref, cos_ref, sin_ref, o_ref):
    j = pl.program_id(1)
    w = w_ref[...].astype(BF16)
    scale = jnp.where(j >= RET_QK_WIDTH // tn, RET_QK_DIM ** -0.5, 1.0).astype(F32)
    half = RET_QK_DIM // 2
    for r, n in _row_chunks(x_ref.shape[0]):
        acc = jnp.dot(x_ref[r:r + n, :], w, preferred_element_type=F32)
        cos = cos_ref[r:r + n, :]
        sin = sin_ref[r:r + n, :]
        for lo in range(0, tn, RET_QK_DIM):
            x1 = acc[:, lo:lo + half]
            x2 = acc[:, lo + half:lo + RET_QK_DIM]
            o_ref[r:r + n, lo:lo + half] = ((x1 * cos - x2 * sin) * scale).astype(o_ref.dtype)
            o_ref[r:r + n, lo + half:lo + RET_QK_DIM] = ((x2 * cos + x1 * sin) * scale).astype(o_ref.dtype)


def _ret_qk(xb, w_stack, layer, cos, sin):
    tn = 512
    n = 2 * RET_QK_WIDTH
    rows = X_ROWS
    return pl.pallas_call(
        functools.partial(_ret_qk_kernel, tn),
        grid=(TOKENS // rows, n // tn),
        in_specs=[_single((rows, D_MODEL), lambda i, j: (i, 0)),
                  _w_cols(w_stack, layer, 0, tn),
                  _single((rows, LANES), lambda i, j: (i, 0)),
                  _single((rows, LANES), lambda i, j: (i, 0))],
        out_specs=pl.BlockSpec((rows, tn), lambda i, j: (i, j)),
        out_shape=jax.ShapeDtypeStruct((TOKENS, n), BF16),
        compiler_params=_params("arbitrary", "arbitrary"),
        name="ret_qk",
    )(xb, w_stack, cos, sin)


def _ffn_up_kernel(x_ref, wg_ref, wv_ref, cwg_ref, cwv_ref, cbg_ref, cbv_ref, slab_ref,
                   o_ref, slab_out_ref, hg_ref, hv_ref):
    _cast_slab(slab_ref, slab_out_ref)
    halo = BF16_SUBLANES
    wg = wg_ref[...].astype(BF16)
    wv = wv_ref[...].astype(BF16)
    for r in range(0, x_ref.shape[0], FFN_ROW_CHUNK):
        n = FFN_ROW_CHUNK
        seq_start = r % SEQ == 0

        def conv(w, cw_ref, cb_ref, h_ref):
            if seq_start:
                h_ref[0:halo, :] = jnp.zeros((halo, h_ref.shape[1]), F32)
                h_ref[halo:, :] = jnp.dot(x_ref[r:r + n, :], w, preferred_element_type=F32)
            else:
                h_ref[...] = jnp.dot(x_ref[r - halo:r + n, :], w, preferred_element_type=F32)
            cw = cw_ref[...]
            out = (cw[2:3] * h_ref[halo:halo + n, :] + cw[1:2] * h_ref[halo - 1:halo - 1 + n, :]
                   + cw[0:1] * h_ref[halo - 2:halo - 2 + n, :])
            return out + cb_ref[...]

        gate = conv(wg, cwg_ref, cbg_ref, hg_ref)
        val = conv(wv, cwv_ref, cbv_ref, hv_ref)
        o_ref[r:r + n, :] = (gate * jax.nn.sigmoid(gate) * val).astype(o_ref.dtype)


def _ffn_up(xb, w_up, conv_w, conv_b, w_down, layer):
    tn = 512
    nj = D_FF // tn
    rows = X_ROWS
    grid = (TOKENS // rows, nj)
    conv_b = conv_b.reshape(DEPTH, 1, 2 * D_FF)
    taps = lambda depth, first: pl.BlockSpec((None, depth, tn), lambda i, j: (layer, 0, first + j))
    slab_src, slab_dst, slab_shape = _slab_specs(w_down, layer, grid)
    return pl.pallas_call(
        _ffn_up_kernel,
        grid=grid,
        in_specs=[_single((rows, D_MODEL), lambda i, j: (i, 0)),
                  _w_cols(w_up, layer, 0, tn), _w_cols(w_up, layer, D_FF, tn),
                  taps(CONV_W, 0), taps(CONV_W, nj), taps(1, 0), taps(1, nj), slab_src],
        out_specs=[pl.BlockSpec((rows, tn), lambda i, j: (i, j)), slab_dst],
        out_shape=[jax.ShapeDtypeStruct((TOKENS, D_FF), BF16), slab_shape],
        scratch_shapes=[pltpu.VMEM((BF16_SUBLANES + FFN_ROW_CHUNK, tn), F32)] * 2,
        compiler_params=_params("arbitrary", "arbitrary"),
        name="ffn_up",
    )(xb, w_up, w_up, conv_w, conv_w, conv_b, conv_b, w_down)


def _mm_res_ln_kernel(a_ref, w_ref, h_ref, g_ref, b_ref, of_ref, ob_ref):
    g = g_ref[...]
    b = b_ref[...]
    for r in range(0, a_ref.shape[0], LN_SUB):
        rows = slice(r, r + LN_SUB)
        y = DEEPNORM_ALPHA * h_ref[rows, :] + jnp.dot(a_ref[rows, :], w_ref[...], preferred_element_type=F32)
        mu = jnp.mean(y, -1, keepdims=True)
        d = y - mu
        var = jnp.mean(d * d, -1, keepdims=True)
        out = d * lax.rsqrt(var + LN_EPS) * g + b
        of_ref[rows, :] = out
        ob_ref[rows, :] = out.astype(BF16)


def _mm_res_ln(a, w, h, g, b, name):
    m, k = a.shape
    n = w.shape[1]
    tm = 512
    row = lambda width: pl.BlockSpec((tm, width), lambda i: (i, 0))
    vec = pl.BlockSpec((1, n), lambda i: (0, 0))
    return pl.pallas_call(
        _mm_res_ln_kernel,
        grid=(m // tm,),
        in_specs=[row(k), _single((k, n), lambda i: (0, 0)), row(n), vec, vec],
        out_specs=[row(n), row(n)],
        out_shape=[jax.ShapeDtypeStruct((m, n), F32), jax.ShapeDtypeStruct((m, n), BF16)],
        compiler_params=_params("arbitrary", vmem=VMEM_LIMIT_LN if k > RET_V_WIDTH else VMEM_LIMIT),
        name=name,
    )(a, w, h, g.reshape(1, n), b.reshape(1, n))


def _ret_core_kernel(lg_ref, q_ref, k_ref, v_ref, g_ref, gn_ref, y_ref, state_ref, decay_ref):
    L = RET_BLOCK
    head = pl.program_id(1)
    step = pl.program_id(2)
    lg = lg_ref[head]

    @pl.when(step == 0)
    def _():
        state_ref[...] = jnp.zeros_like(state_ref)
        n = lax.broadcasted_iota(jnp.int32, (L, L), 0)
        m = lax.broadcasted_iota(jnp.int32, (L, L), 1)
        dist = jnp.abs(n - m).astype(F32)
        decay_ref[...] = jnp.where((m // CHUNK) <= (n // CHUNK), jnp.exp(lg * dist), 0.0)

    idx = lax.broadcasted_iota(jnp.int32, (L, 1), 0).astype(F32)
    q_decay = jnp.exp(idx * lg)
    k_decay = jnp.exp((L - idx) * lg)
    block_decay = jnp.exp(jnp.full((1, RET_V_DIM), L, F32) * lg)
    decay = decay_ref[...]
    gn = gn_ref[...]
    state = state_ref[...]
    for r in range(0, q_ref.shape[0], L):
        rows = slice(r, r + L)
        q = q_ref[rows, :]
        k = k_ref[rows, :]
        v = v_ref[rows, :]
        s = lax.dot_general(q, k, (((1,), (1,)), ((), ())), preferred_element_type=F32)
        o = jnp.dot((s * decay).astype(BF16), v, preferred_element_type=F32)
        o = o + jnp.dot(q, state.astype(BF16), preferred_element_type=F32) * q_decay
        k_dec = (k.astype(F32) * k_decay).astype(BF16)
        upd = lax.dot_general(k_dec, v, (((0,), (0,)), ((), ())), preferred_element_type=F32)
        state = state * block_decay + upd

        mu = jnp.mean(o, -1, keepdims=True)
        d = o - mu
        var = jnp.mean(d * d, -1, keepdims=True)
        normed = d * lax.rsqrt(var + LN_EPS) * gn
        gate = g_ref[rows, :].astype(F32)
        y_ref[rows, :] = (gate * jax.nn.sigmoid(gate) * normed).astype(y_ref.dtype)
    state_ref[...] = state


def _ret_core(qk, vg, gn_g):
    ts = SEQ
    ns = SEQ // ts
    log_gamma = jnp.log(1.0 - 2.0 ** (-5.0 - jnp.arange(RET_HEADS, dtype=F32)))
    return pl.pallas_call(
        _ret_core_kernel,
        grid=(BATCH, RET_HEADS, ns),
        in_specs=[pl.BlockSpec(memory_space=pltpu.SMEM),
                  pl.BlockSpec((ts, RET_QK_DIM), lambda b, h, c: (b * ns + c, h)),
                  pl.BlockSpec((ts, RET_QK_DIM), lambda b, h, c: (b * ns + c, RET_HEADS + h)),
                  pl.BlockSpec((ts, RET_V_DIM), lambda b, h, c: (b * ns + c, h)),
                  pl.BlockSpec((ts, RET_V_DIM), lambda b, h, c: (b * ns + c, RET_HEADS + h)),
                  pl.BlockSpec((1, RET_V_DIM), lambda b, h, c: (0, h))],
        out_specs=pl.BlockSpec((ts, RET_V_DIM), lambda b, h, c: (b * ns + c, h)),
        out_shape=jax.ShapeDtypeStruct((TOKENS, RET_V_WIDTH), BF16),
        scratch_shapes=[pltpu.VMEM((RET_QK_DIM, RET_V_DIM), F32),
                        pltpu.VMEM((RET_BLOCK, RET_BLOCK), F32)],
        compiler_params=_params("arbitrary", "arbitrary", "arbitrary"),
        name="ret_core",
    )(log_gamma, qk, qk, vg, vg, gn_g.reshape(1, RET_V_WIDTH))


def _rms(x, g):
    return x * lax.rsqrt(jnp.mean(x * x, -1, keepdims=True) + RMS_EPS) * g


def _mla_proj_kernel(x_ref, w_in_ref, qg_ref, kvg_ref, w_uq_ref, w_uk_ref, w_uv_ref,
                     c_ref, sp_ref, sn_ref, slab_ref, q_ref, k_ref, v_ref, slab_out_ref):
    _cast_slab(slab_ref, slab_out_ref)
    c = c_ref[...]
    sp = sp_ref[...]
    sn = sn_ref[...]
    proj = jnp.dot(x_ref[...], w_in_ref[...], preferred_element_type=F32)
    c_q = _rms(proj[:, :MLA_Q_RANK], qg_ref[...]).astype(BF16)
    c_kv = _rms(proj[:, MLA_Q_RANK:MLA_Q_RANK + MLA_KV_RANK], kvg_ref[...]).astype(BF16)
    k_rope = _rope_mla(proj[:, MLA_Q_RANK + MLA_KV_RANK:], c, sp, sn).astype(BF16)

    q = jnp.dot(c_q, w_uq_ref[...], preferred_element_type=F32)
    k_nope = jnp.dot(c_kv, w_uk_ref[...], preferred_element_type=F32)
    for hh in range(MLA_HEADS):
        lo = hh * MLA_QK_PAD
        q_ref[:, lo:lo + MLA_NOPE] = (q[:, lo:lo + MLA_NOPE] * MLA_Q_SCALE).astype(BF16)
        q_ref[:, lo + MLA_NOPE:lo + MLA_QK_PAD] = (_rope_mla(
            q[:, lo + MLA_NOPE:lo + MLA_QK_PAD], c, sp, sn) * MLA_Q_SCALE).astype(BF16)
        k_ref[:, lo:lo + MLA_NOPE] = k_nope[:, hh * MLA_NOPE:(hh + 1) * MLA_NOPE].astype(BF16)
        k_ref[:, lo + MLA_NOPE:lo + MLA_QK_PAD] = k_rope
    v_ref[...] = jnp.dot(c_kv, w_uv_ref[...], preferred_element_type=F32).astype(BF16)


def _mla_proj(xb, w_in, qg, kvg, w_uq, w_uk, w_uv, c, sp, sn, w_out, layer):
    tm = 512
    qk_w = MLA_HEADS * MLA_QK_PAD
    v_w = MLA_HEADS * MLA_V
    grid = (TOKENS // tm,)
    row = lambda w: pl.BlockSpec((tm, w), lambda i: (i, 0))
    res = lambda a: _single(a.shape, lambda i: (0, 0))
    slab_src, slab_dst, slab_shape = _slab_specs(w_out, layer, grid)
    return pl.pallas_call(
        _mla_proj_kernel,
        grid=grid,
        in_specs=[row(D_MODEL), res(w_in), res(qg), res(kvg), res(w_uq), res(w_uk), res(w_uv),
                  row(LANES), row(LANES), row(LANES), slab_src],
        out_specs=[row(qk_w), row(qk_w), row(v_w), slab_dst],
        out_shape=[jax.ShapeDtypeStruct((TOKENS, qk_w), BF16),
                   jax.ShapeDtypeStruct((TOKENS, qk_w), BF16),
                   jax.ShapeDtypeStruct((TOKENS, v_w), BF16), slab_shape],
        compiler_params=_params("arbitrary"),
        name="mla_proj",
    )(xb, w_in, qg, kvg, w_uq, w_uk, w_uv, c, sp, sn, w_out)


def _mla_attn_kernel(tq, q_ref, k_ref, v_ref, o_ref, s_ref):
    dims = (((1,), (1,)), ((), ()))
    row = lax.broadcasted_iota(jnp.int32, (tq, tq), 0)
    col = lax.broadcasted_iota(jnp.int32, (tq, tq), 1)
    visible = (row // CHUNK) >= (col // CHUNK)
    for c in range(SEQ // tq):
        lo = c * tq
        n = lo + tq
        for hh in range(MLA_HEADS_PER_STEP):
            qk = slice(hh * MLA_QK_PAD, (hh + 1) * MLA_QK_PAD)
            vc = slice(hh * MLA_V, (hh + 1) * MLA_V)
            buf = s_ref.at[MLA_HEADS_PER_STEP * (c % 2) + hh]
            buf[:, 0:n] = lax.dot_general(q_ref[lo:n, qk], k_ref[0:n, qk], dims, preferred_element_type=F32)
            buf[:, lo:n] = jnp.where(visible, buf[:, lo:n], NEG_INF)
            s = buf[:, 0:n]
            p = jnp.exp2(s - jnp.max(s, -1, keepdims=True))
            l = jnp.sum(p, -1, keepdims=True)
            acc = jnp.dot(p.astype(BF16), v_ref[0:n, vc], preferred_element_type=F32)
            o_ref[lo:n, vc] = (acc * (1.0 / l)).astype(o_ref.dtype)


def _mla_attn(q, k, v):
    tq = 256
    hp = MLA_HEADS_PER_STEP
    return pl.pallas_call(
        functools.partial(_mla_attn_kernel, tq),
        grid=(BATCH, MLA_HEADS // hp),
        in_specs=[pl.BlockSpec((SEQ, hp * MLA_QK_PAD), lambda b, h: (b, h)),
                  pl.BlockSpec((SEQ, hp * MLA_QK_PAD), lambda b, h: (b, h)),
                  pl.BlockSpec((SEQ, hp * MLA_V), lambda b, h: (b, h))],
        out_specs=pl.BlockSpec((SEQ, hp * MLA_V), lambda b, h: (b, h)),
        out_shape=jax.ShapeDtypeStruct((TOKENS, MLA_HEADS * MLA_V), BF16),
        scratch_shapes=[pltpu.VMEM((2 * hp, tq, SEQ), F32)],
        compiler_params=_params("arbitrary", "arbitrary"),
        name="mla_attn",
    )(q, k, v)


def _xa_attn_kernel(q_ref, kv_ref, o_ref):
    for hh in range(XA_HEADS):
        cols = slice(hh * XA_DIM, (hh + 1) * XA_DIM)
        vcols = slice(D_MODEL + hh * XA_DIM, D_MODEL + (hh + 1) * XA_DIM)
        s = lax.dot_general(q_ref[:, cols], kv_ref[:, cols], (((1,), (1,)), ((), ())),
                            preferred_element_type=F32) * (XA_DIM ** -0.5)
        e = jnp.exp(s - jnp.max(s, -1, keepdims=True))
        p = e / jnp.sum(e, -1, keepdims=True)
        o_ref[:, cols] = jnp.dot(p.astype(BF16), kv_ref[:, vcols],
                                 preferred_element_type=F32).astype(o_ref.dtype)


def _xa_attn(q, kv):
    tq = 1024
    nq = SEQ // tq
    return pl.pallas_call(
        _xa_attn_kernel,
        grid=(BATCH, nq),
        in_specs=[pl.BlockSpec((tq, D_MODEL), lambda b, i: (b * nq + i, 0)),
                  pl.BlockSpec((MEM_LEN, 2 * D_MODEL), lambda b, i: (b, 0))],
        out_specs=pl.BlockSpec((tq, D_MODEL), lambda b, i: (b * nq + i, 0)),
        out_shape=jax.ShapeDtypeStruct((TOKENS, D_MODEL), BF16),
        compiler_params=_params("arbitrary", "arbitrary"),
        name="xa_attn",
    )(q, kv)


def _mla_weights(w_in, w_uq, w_ukv):
    pad = LANES - MLA_ROPE
    w_in = jnp.pad(w_in, ((0, 0), (0, pad)))
    w_uq = w_uq.reshape(MLA_Q_RANK, MLA_HEADS, MLA_NOPE + MLA_ROPE)
    w_uq = jnp.pad(w_uq, ((0, 0), (0, 0), (0, pad))).reshape(MLA_Q_RANK, MLA_HEADS * MLA_QK_PAD)
    w_ukv = w_ukv.reshape(MLA_KV_RANK, MLA_HEADS, MLA_NOPE + MLA_V)
    w_uk = w_ukv[:, :, :MLA_NOPE].reshape(MLA_KV_RANK, MLA_HEADS * MLA_NOPE)
    w_uv = w_ukv[:, :, MLA_NOPE:].reshape(MLA_KV_RANK, MLA_HEADS * MLA_V)
    return w_in.astype(BF16), w_uq.astype(BF16), w_uk.astype(BF16), w_uv.astype(BF16)


def kernel(x, mem, positions, ret_w_in, ret_gn_g, ret_w_out, mla_w_in, mla_q_norm_g, mla_w_uq, mla_kv_norm_g, mla_w_ukv, mla_w_out, xa_w_q, xa_w_kv, xa_w_out, ffn_w_up, ffn_conv_w, ffn_conv_b, ffn_w_down, ln_mix_g, ln_mix_b, ln_mem_g, ln_mem_b, ln_ffn_g, ln_ffn_b):
    h = x.reshape(TOKENS, D_MODEL)
    mem_b = mem.reshape(BATCH * MEM_LEN, D_MODEL).astype(BF16)
    cos_r, sin_r, c_m, sp_m, sn_m, hb = _rope_tables(positions, h)

    for layer in range(DEPTH):
        j = layer // N_MIXERS
        if layer % N_MIXERS == 0:
            qk = _ret_qk(hb, ret_w_in, j, cos_r, sin_r)
            vg, w_out = _xres_matmul(hb, ret_w_in, j, 2 * RET_QK_WIDTH, 2 * RET_V_WIDTH, tn=512,
                                     name="ret_vg", cast=(ret_w_out, j))
            mix_in = _ret_core(qk, vg, ret_gn_g[j])
        else:
            w_in, w_uq, w_uk, w_uv = _mla_weights(mla_w_in[j], mla_w_uq[j], mla_w_ukv[j])
            q, k, v, w_out = _mla_proj(hb, w_in, mla_q_norm_g[j].reshape(1, MLA_Q_RANK),
                                       mla_kv_norm_g[j].reshape(1, MLA_KV_RANK), w_uq, w_uk, w_uv,
                                       c_m, sp_m, sn_m, mla_w_out, j)
            mix_in = _mla_attn(q, k, v)
        h, hb = _mm_res_ln(mix_in, w_out, h, ln_mix_g[layer], ln_mix_b[layer], name="mix_out_ln")

        xq, xa_out_w = _xres_matmul(hb, xa_w_q, layer, 0, D_MODEL, tn=512, name="xa_q", cast=(xa_w_out, layer))
        xkv = _xres_matmul(mem_b, xa_w_kv, layer, 0, 2 * D_MODEL, tn=512, name="xa_kv")
        xo = _xa_attn(xq, xkv)
        h, hb = _mm_res_ln(xo, xa_out_w, h, ln_mem_g[layer], ln_mem_b[layer], name="xa_out_ln")

        act, w_down = _ffn_up(hb, ffn_w_up, ffn_conv_w, ffn_conv_b, ffn_w_down, layer)
        h, hb = _mm_res_ln(act, w_down, h, ln_ffn_g[layer], ln_ffn_b[layer], name="ffn_down_ln")

    return h.reshape(BATCH, SEQ, D_MODEL)
```

```python
import functools

import jax
import jax.numpy as jnp
from jax import lax
from jax.experimental import pallas as pl
from jax.experimental.pallas import tpu as pltpu

D_MODEL = 2048
BATCH = 4
SEQ = 2048
DEPTH = 4
CHUNK = 64
MEM_LEN = 256
N_MIXERS = 2

RET_HEADS = 8
RET_QK_DIM = D_MODEL // RET_HEADS
RET_V_DIM = 2 * D_MODEL // RET_HEADS
RET_QK_WIDTH = RET_HEADS * RET_QK_DIM
RET_V_WIDTH = RET_HEADS * RET_V_DIM

MLA_HEADS = 16
MLA_Q_RANK = 512
MLA_KV_RANK = 512
MLA_NOPE = 128
MLA_ROPE = 64
MLA_V = 128

XA_HEADS = 4
XA_DIM = D_MODEL // XA_HEADS

D_FF = 5632
CONV_W = 3

ROPE_BASE = 10000.0
LN_EPS = 1e-5
RMS_EPS = 1e-6
NEG_INF = -1e30
DEEPNORM_ALPHA = (2 * DEPTH) ** 0.25
LOG2_E = 1.4426950408889634
MLA_Q_SCALE = (MLA_NOPE + MLA_ROPE) ** -0.5 * LOG2_E
MLA_HEADS_PER_STEP = 2

TOKENS = BATCH * SEQ
LANES = 128
BF16_SUBLANES = 16
MLA_QK_PAD = 256
RET_BLOCK = 256
V7X_VMEM_BYTES = 64 * 1024 * 1024
VMEM_LIMIT = V7X_VMEM_BYTES * 7 // 8
VMEM_LIMIT_LN = V7X_VMEM_BYTES * 15 // 16

X_ROWS = 2 * SEQ
ROW_CHUNK = 1024
FFN_ROW_CHUNK = 512
LN_SUB = 256

F32 = jnp.float32
BF16 = jnp.bfloat16


def _params(*semantics, vmem=VMEM_LIMIT):
    return pltpu.CompilerParams(dimension_semantics=semantics, vmem_limit_bytes=vmem)


def _single(shape, index_map):
    return pl.BlockSpec(shape, index_map, pipeline_mode=pl.Buffered(1))


def _rope_tables_kernel(pos_ref, invf_ret_ref, invf_mla_ref, x_ref, cos_r, sin_r, c_m, sp_m, sn_m, xb_ref):
    xb_ref[...] = x_ref[...].astype(BF16)
    pos = pos_ref[...].astype(F32)
    ang = pos * invf_ret_ref[...]
    cos_r[...] = jnp.cos(ang)
    sin_r[...] = jnp.sin(ang)
    angm = pos * invf_mla_ref[...]
    lane = lax.broadcasted_iota(jnp.int32, angm.shape, 1)
    half = MLA_ROPE // 2
    c = jnp.cos(angm)
    s = jnp.sin(angm)
    c_m[...] = jnp.where(lane < MLA_ROPE, c, 0.0)
    sp_m[...] = jnp.where((lane >= half) & (lane < MLA_ROPE), s, 0.0)
    sn_m[...] = jnp.where(lane < half, -s, 0.0)


def _rope_tables(positions, x):
    tm = 1024
    pos = positions.reshape(TOKENS, 1)
    invf_ret = ROPE_BASE ** (-jnp.arange(0, RET_QK_DIM, 2, dtype=F32) / RET_QK_DIM)
    invf_mla = ROPE_BASE ** (-jnp.arange(0, MLA_ROPE, 2, dtype=F32) / MLA_ROPE)
    invf_mla = jnp.concatenate([invf_mla, invf_mla, jnp.zeros((LANES - MLA_ROPE,), F32)])
    row = pl.BlockSpec((tm, LANES), lambda i: (i, 0))
    const = pl.BlockSpec((1, LANES), lambda i: (0, 0))
    wide = pl.BlockSpec((tm, D_MODEL), lambda i: (i, 0))
    return pl.pallas_call(
        _rope_tables_kernel,
        grid=(TOKENS // tm,),
        in_specs=[pl.BlockSpec((tm, 1), lambda i: (i, 0)), const, const, wide],
        out_specs=[row] * 5 + [wide],
        out_shape=[jax.ShapeDtypeStruct((TOKENS, LANES), F32)] * 5 + [jax.ShapeDtypeStruct((TOKENS, D_MODEL), BF16)],
        compiler_params=_params("arbitrary"),
        name="rope_tables",
    )(pos, invf_ret.reshape(1, LANES), invf_mla.reshape(1, LANES), x)


def _rope_mla(x, c, sp, sn):
    half = MLA_ROPE // 2
    return x * c + pltpu.roll(x, half, 1) * sp + pltpu.roll(x, LANES - half, 1) * sn


def _row_chunks(rows):
    chunk = min(ROW_CHUNK, rows)
    return [(r, chunk) for r in range(0, rows, chunk)]


def _cast_slab(src_ref, dst_ref):
    dst_ref[...] = src_ref[...].astype(BF16)


def _slab_specs(w_stack, layer, grid):
    k, n = w_stack.shape[1:]
    steps = 1
    for g in grid:
        steps *= g
    rows = k // steps
    assert rows * steps == k and rows % BF16_SUBLANES == 0
    if len(grid) == 1:
        flat = lambda i: i
    else:
        flat = lambda i, j: i * grid[1] + j
    src = pl.BlockSpec((None, rows, n), lambda *ids: (layer, flat(*ids), 0))
    dst = pl.BlockSpec((rows, n), lambda *ids: (flat(*ids), 0))
    return src, dst, jax.ShapeDtypeStruct((k, n), BF16)


def _xres_matmul_kernel(x_ref, w_ref, o_ref):
    w = w_ref[...].astype(BF16)
    for r, n in _row_chunks(x_ref.shape[0]):
        o_ref[r:r + n, :] = jnp.dot(x_ref[r:r + n, :], w, preferred_element_type=F32).astype(o_ref.dtype)


def _xres_matmul_cast_kernel(x_ref, w_ref, slab_ref, o_ref, slab_out_ref):
    _cast_slab(slab_ref, slab_out_ref)
    _xres_matmul_kernel(x_ref, w_ref, o_ref)


def _w_cols(w_stack, layer, col0, tn):
    k = w_stack.shape[1]
    first = col0 // tn
    return pl.BlockSpec((None, k, tn), lambda i, j: (layer, 0, first + j))


def _xres_matmul(x, w_stack, layer, col0, n, tn, name, cast=None):
    m, k = x.shape
    rows = min(X_ROWS, m)
    grid = (m // rows, n // tn)
    in_specs = [_single((rows, k), lambda i, j: (i, 0)), _w_cols(w_stack, layer, col0, tn)]
    out_specs = [pl.BlockSpec((rows, tn), lambda i, j: (i, j))]
    out_shape = [jax.ShapeDtypeStruct((m, n), BF16)]
    args = [x, w_stack]
    body = _xres_matmul_kernel
    if cast is not None:
        src, dst, shape = _slab_specs(cast[0], cast[1], grid)
        in_specs.append(src)
        out_specs.append(dst)
        out_shape.append(shape)
        args.append(cast[0])
        body = _xres_matmul_cast_kernel
    out = pl.pallas_call(
        body,
        grid=grid,
        in_specs=in_specs,
        out_specs=out_specs,
        out_shape=out_shape,
        compiler_params=_params("arbitrary", "arbitrary"),
        name=name,
    )(*args)
    return out if cast is not None else out[0]


def _ret_qk_kernel(tn, x_ref, w_ref, cos_ref, sin_ref, o_ref):
    j = pl.program_id(1)
    w = w_ref[...].astype(BF16)
    scale = jnp.where(j >= RET_QK_WIDTH // tn, RET_QK_DIM ** -0.5, 1.0).astype(F32)
    half = RET_QK_DIM // 2
    for r, n in _row_chunks(x_ref.shape[0]):
        acc = jnp.dot(x_ref[r:r + n, :], w, preferred_element_type=F32)
        cos = cos_ref[r:r + n, :]
        sin = sin_ref[r:r + n, :]
        for lo in range(0, tn, RET_QK_DIM):
            x1 = acc[:, lo:lo + half]
            x2 = acc[:, lo + half:lo + RET_QK_DIM]
            o_ref[r:r + n, lo:lo + half] = ((x1 * cos - x2 * sin) * scale).astype(o_ref.dtype)
            o_ref[r:r + n, lo + half:lo + RET_QK_DIM] = ((x2 * cos + x1 * sin) * scale).astype(o_ref.dtype)


def _ret_qk(xb, w_stack, layer, cos, sin):
    tn = 512
    n = 2 * RET_QK_WIDTH
    rows = X_ROWS
    return pl.pallas_call(
        functools.partial(_ret_qk_kernel, tn),
        grid=(TOKENS // rows, n // tn),
        in_specs=[_single((rows, D_MODEL), lambda i, j: (i, 0)),
                  _w_cols(w_stack, layer, 0, tn),
                  _single((rows, LANES), lambda i, j: (i, 0)),
                  _single((rows, LANES), lambda i, j: (i, 0))],
        out_specs=pl.BlockSpec((rows, tn), lambda i, j: (i, j)),
        out_shape=jax.ShapeDtypeStruct((TOKENS, n), BF16),
        compiler_params=_params("arbitrary", "arbitrary"),
        name="ret_qk",
    )(xb, w_stack, cos, sin)


def _ffn_up_kernel(x_ref, wg_ref, wv_ref, cwg_ref, cwv_ref, cbg_ref, cbv_ref, slab_ref,
                   o_ref, slab_out_ref, hg_ref, hv_ref):
    _cast_slab(slab_ref, slab_out_ref)
    halo = BF16_SUBLANES
    wg = wg_ref[...].astype(BF16)
    wv = wv_ref[...].astype(BF16)
    for r in range(0, x_ref.shape[0], FFN_ROW_CHUNK):
        n = FFN_ROW_CHUNK
        seq_start = r % SEQ == 0

        def conv(w, cw_ref, cb_ref, h_ref):
            if seq_start:
                h_ref[0:halo, :] = jnp.zeros((halo, h_ref.shape[1]), F32)
                h_ref[halo:, :] = jnp.dot(x_ref[r:r + n, :], w, preferred_element_type=F32)
            else:
                h_ref[...] = jnp.dot(x_ref[r - halo:r + n, :], w, preferred_element_type=F32)
            cw = cw_ref[...]
            out = (cw[2:3] * h_ref[halo:halo + n, :] + cw[1:2] * h_ref[halo - 1:halo - 1 + n, :]
                   + cw[0:1] * h_ref[halo - 2:halo - 2 + n, :])
            return out + cb_ref[...]

        gate = conv(wg, cwg_ref, cbg_ref, hg_ref)
        val = conv(wv, cwv_ref, cbv_ref, hv_ref)
        o_ref[r:r + n, :] = (gate * jax.nn.sigmoid(gate) * val).astype(o_ref.dtype)


def _ffn_up(xb, w_up, conv_w, conv_b, w_down, layer):
    tn = 512
    nj = D_FF // tn
    rows = X_ROWS
    grid = (TOKENS // rows, nj)
    conv_b = conv_b.reshape(DEPTH, 1, 2 * D_FF)
    taps = lambda depth, first: pl.BlockSpec((None, depth, tn), lambda i, j: (layer, 0, first + j))
    slab_src, slab_dst, slab_shape = _slab_specs(w_down, layer, grid)
    return pl.pallas_call(
        _ffn_up_kernel,
        grid=grid,
        in_specs=[_single((rows, D_MODEL), lambda i, j: (i, 0)),
                  _w_cols(w_up, layer, 0, tn), _w_cols(w_up, layer, D_FF, tn),
                  taps(CONV_W, 0), taps(CONV_W, nj), taps(1, 0), taps(1, nj), slab_src],
        out_specs=[pl.BlockSpec((rows, tn), lambda i, j: (i, j)), slab_dst],
        out_shape=[jax.ShapeDtypeStruct((TOKENS, D_FF), BF16), slab_shape],
        scratch_shapes=[pltpu.VMEM((BF16_SUBLANES + FFN_ROW_CHUNK, tn), F32)] * 2,
        compiler_params=_params("arbitrary", "arbitrary"),
        name="ffn_up",
    )(xb, w_up, w_up, conv_w, conv_w, conv_b, conv_b, w_down)


def _mm_res_ln_kernel(a_ref, w_ref, h_ref, g_ref, b_ref, of_ref, ob_ref):
    g = g_ref[...]
    b = b_ref[...]
    for r in range(0, a_ref.shape[0], LN_SUB):
        rows = slice(r, r + LN_SUB)
        y = DEEPNORM_ALPHA * h_ref[rows, :] + jnp.dot(a_ref[rows, :], w_ref[...], preferred_element_type=F32)
        mu = jnp.mean(y, -1, keepdims=True)
        d = y - mu
        var = jnp.mean(d * d, -1, keepdims=True)
        out = d * lax.rsqrt(var + LN_EPS) * g + b
        of_ref[rows, :] = out
        ob_ref[rows, :] = out.astype(BF16)


def _mm_res_ln(a, w, h, g, b, name):
    m, k = a.shape
    n = w.shape[1]
    tm = 512
    row = lambda width: pl.BlockSpec((tm, width), lambda i: (i, 0))
    vec = pl.BlockSpec((1, n), lambda i: (0, 0))
    return pl.pallas_call(
        _mm_res_ln_kernel,
        grid=(m // tm,),
        in_specs=[row(k), _single((k, n), lambda i: (0, 0)), row(n), vec, vec],
        out_specs=[row(n), row(n)],
        out_shape=[jax.ShapeDtypeStruct((m, n), F32), jax.ShapeDtypeStruct((m, n), BF16)],
        compiler_params=_params("arbitrary", vmem=VMEM_LIMIT_LN if k > RET_V_WIDTH else VMEM_LIMIT),
        name=name,
    )(a, w, h, g.reshape(1, n), b.reshape(1, n))


def _ret_core_kernel(lg_ref, q_ref, k_ref, v_ref, g_ref, gn_ref, y_ref, state_ref, decay_ref):
    L = RET_BLOCK
    head = pl.program_id(1)
    step = pl.program_id(2)
    lg = lg_ref[head]

    @pl.when(step == 0)
    def _():
        state_ref[...] = jnp.zeros_like(state_ref)
        n = lax.broadcasted_iota(jnp.int32, (L, L), 0)
        m = lax.broadcasted_iota(jnp.int32, (L, L), 1)
        dist = jnp.abs(n - m).astype(F32)
        decay_ref[...] = jnp.where((m // CHUNK) <= (n // CHUNK), jnp.exp(lg * dist), 0.0)

    idx = lax.broadcasted_iota(jnp.int32, (L, 1), 0).astype(F32)
    q_decay = jnp.exp(idx * lg)
    k_decay = jnp.exp((L - idx) * lg)
    block_decay = jnp.exp(jnp.full((1, RET_V_DIM), L, F32) * lg)
    decay = decay_ref[...]
    gn = gn_ref[...]
    state = state_ref[...]
    for r in range(0, q_ref.shape[0], L):
        rows = slice(r, r + L)
        q = q_ref[rows, :]
        k = k_ref[rows, :]
        v = v_ref[rows, :]
        s = lax.dot_general(q, k, (((1,), (1,)), ((), ())), preferred_element_type=F32)
        o = jnp.dot((s * decay).astype(BF16), v, preferred_element_type=F32)
        o = o + jnp.dot(q, state.astype(BF16), preferred_element_type=F32) * q_decay
        k_dec = (k.astype(F32) * k_decay).astype(BF16)
        upd = lax.dot_general(k_dec, v, (((0,), (0,)), ((), ())), preferred_element_type=F32)
        state = state * block_decay + upd

        mu = jnp.mean(o, -1, keepdims=True)
        d = o - mu
        var = jnp.mean(d * d, -1, keepdims=True)
        normed = d * lax.rsqrt(var + LN_EPS) * gn
        gate = g_ref[rows, :].astype(F32)
        y_ref[rows, :] = (gate * jax.nn.sigmoid(gate) * normed).astype(y_ref.dtype)
    state_ref[...] = state


def _ret_core(qk, vg, gn_g):
    ts = SEQ
    ns = SEQ // ts
    log_gamma = jnp.log(1.0 - 2.0 ** (-5.0 - jnp.arange(RET_HEADS, dtype=F32)))
    return pl.pallas_call(
        _ret_core_kernel,
        grid=(BATCH, RET_HEADS, ns),
        in_specs=[pl.BlockSpec(memory_space=pltpu.SMEM),
                  pl.BlockSpec((ts, RET_QK_DIM), lambda b, h, c: (b * ns + c, h)),
                  pl.BlockSpec((ts, RET_QK_DIM), lambda b, h, c: (b * ns + c, RET_HEADS + h)),
                  pl.BlockSpec((ts, RET_V_DIM), lambda b, h, c: (b * ns + c, h)),
                  pl.BlockSpec((ts, RET_V_DIM), lambda b, h, c: (b * ns + c, RET_HEADS + h)),
                  pl.BlockSpec((1, RET_V_DIM), lambda b, h, c: (0, h))],
        out_specs=pl.BlockSpec((ts, RET_V_DIM), lambda b, h, c: (b * ns + c, h)),
        out_shape=jax.ShapeDtypeStruct((TOKENS, RET_V_WIDTH), BF16),
        scratch_shapes=[pltpu.VMEM((RET_QK_DIM, RET_V_DIM), F32),
                        pltpu.VMEM((RET_BLOCK, RET_BLOCK), F32)],
        compiler_params=_params("arbitrary", "arbitrary", "arbitrary"),
        name="ret_core",
    )(log_gamma, qk, qk, vg, vg, gn_g.reshape(1, RET_V_WIDTH))


def _rms(x, g):
    return x * lax.rsqrt(jnp.mean(x * x, -1, keepdims=True) + RMS_EPS) * g


def _mla_proj_kernel(x_ref, w_in_ref, qg_ref, kvg_ref, w_uq_ref, w_uk_ref, w_uv_ref,
                     c_ref, sp_ref, sn_ref, slab_ref, q_ref, k_ref, v_ref, slab_out_ref):
    _cast_slab(slab_ref, slab_out_ref)
    c = c_ref[...]
    sp = sp_ref[...]
    sn = sn_ref[...]
    proj = jnp.dot(x_ref[...], w_in_ref[...], preferred_element_type=F32)
    c_q = _rms(proj[:, :MLA_Q_RANK], qg_ref[...]).astype(BF16)
    c_kv = _rms(proj[:, MLA_Q_RANK:MLA_Q_RANK + MLA_KV_RANK], kvg_ref[...]).astype(BF16)
    k_rope = _rope_mla(proj[:, MLA_Q_RANK + MLA_KV_RANK:], c, sp, sn).astype(BF16)

    q = jnp.dot(c_q, w_uq_ref[...], preferred_element_type=F32)
    k_nope = jnp.dot(c_kv, w_uk_ref[...], preferred_element_type=F32)
    for hh in range(MLA_HEADS):
        lo = hh * MLA_QK_PAD
        q_ref[:, lo:lo + MLA_NOPE] = (q[:, lo:lo + MLA_NOPE] * MLA_Q_SCALE).astype(BF16)
        q_ref[:, lo + MLA_NOPE:lo + MLA_QK_PAD] = (_rope_mla(
            q[:, lo + MLA_NOPE:lo + MLA_QK_PAD], c, sp, sn) * MLA_Q_SCALE).astype(BF16)
        k_ref[:, lo:lo + MLA_NOPE] = k_nope[:, hh * MLA_NOPE:(hh + 1) * MLA_NOPE].astype(BF16)
        k_ref[:, lo + MLA_NOPE:lo + MLA_QK_PAD] = k_rope
    v_ref[...] = jnp.dot(c_kv, w_uv_ref[...], preferred_element_type=F32).astype(BF16)


def _mla_proj(xb, w_in, qg, kvg, w_uq, w_uk, w_uv, c, sp, sn, w_out, layer):
    tm = 512
    qk_w = MLA_HEADS * MLA_QK_PAD
    v_w = MLA_HEADS * MLA_V
    grid = (TOKENS // tm,)
    row = lambda w: pl.BlockSpec((tm, w), lambda i: (i, 0))
    res = lambda a: _single(a.shape, lambda i: (0, 0))
    slab_src, slab_dst, slab_shape = _slab_specs(w_out, layer, grid)
    return pl.pallas_call(
        _mla_proj_kernel,
        grid=grid,
        in_specs=[row(D_MODEL), res(w_in), res(qg), res(kvg), res(w_uq), res(w_uk), res(w_uv),
                  row(LANES), row(LANES), row(LANES), slab_src],
        out_specs=[row(qk_w), row(qk_w), row(v_w), slab_dst],
        out_shape=[jax.ShapeDtypeStruct((TOKENS, qk_w), BF16),
                   jax.ShapeDtypeStruct((TOKENS, qk_w), BF16),
                   jax.ShapeDtypeStruct((TOKENS, v_w), BF16), slab_shape],
        compiler_params=_params("arbitrary"),
        name="mla_proj",
    )(xb, w_in, qg, kvg, w_uq, w_uk, w_uv, c, sp, sn, w_out)


def _mla_attn_kernel(tq, q_ref, k_ref, v_ref, o_ref, s_ref):
    dims = (((1,), (1,)), ((), ()))
    row = lax.broadcasted_iota(jnp.int32, (tq, tq), 0)
    col = lax.broadcasted_iota(jnp.int32, (tq, tq), 1)
    visible = (row // CHUNK) >= (col // CHUNK)
    for c in range(SEQ // tq):
        lo = c * tq
        n = lo + tq
        for hh in range(MLA_HEADS_PER_STEP):
            qk = slice(hh * MLA_QK_PAD, (hh + 1) * MLA_QK_PAD)
            vc = slice(hh * MLA_V, (hh + 1) * MLA_V)
            buf = s_ref.at[MLA_HEADS_PER_STEP * (c % 2) + hh]
            buf[:, 0:n] = lax.dot_general(q_ref[lo:n, qk], k_ref[0:n, qk], dims, preferred_element_type=F32)
            buf[:, lo:n] = jnp.where(visible, buf[:, lo:n], NEG_INF)
            s = buf[:, 0:n]
            p = jnp.exp2(s - jnp.max(s, -1, keepdims=True))
            l = jnp.sum(p, -1, keepdims=True)
            acc = jnp.dot(p.astype(BF16), v_ref[0:n, vc], preferred_element_type=F32)
            o_ref[lo:n, vc] = (acc * (1.0 / l)).astype(o_ref.dtype)


def _mla_attn(q, k, v):
    tq = 256
    hp = MLA_HEADS_PER_STEP
    return pl.pallas_call(
        functools.partial(_mla_attn_kernel, tq),
        grid=(BATCH, MLA_HEADS // hp),
        in_specs=[pl.BlockSpec((SEQ, hp * MLA_QK_PAD), lambda b, h: (b, h)),
                  pl.BlockSpec((SEQ, hp * MLA_QK_PAD), lambda b, h: (b, h)),
                  pl.BlockSpec((SEQ, hp * MLA_V), lambda b, h: (b, h))],
        out_specs=pl.BlockSpec((SEQ, hp * MLA_V), lambda b, h: (b, h)),
        out_shape=jax.ShapeDtypeStruct((TOKENS, MLA_HEADS * MLA_V), BF16),
        scratch_shapes=[pltpu.VMEM((2 * hp, tq, SEQ), F32)],
        compiler_params=_params("arbitrary", "arbitrary"),
        name="mla_attn",
    )(q, k, v)


def _xa_attn_kernel(q_ref, kv_ref, o_ref):
    for hh in range(XA_HEADS):
        cols = slice(hh * XA_DIM, (hh + 1) * XA_DIM)
        vcols = slice(D_MODEL + hh * XA_DIM, D_MODEL + (hh + 1) * XA_DIM)
        s = lax.dot_general(q_ref[:, cols], kv_ref[:, cols], (((1,), (1,)), ((), ())),
                            preferred_element_type=F32) * (XA_DIM ** -0.5)
        e = jnp.exp(s - jnp.max(s, -1, keepdims=True))
        p = e / jnp.sum(e, -1, keepdims=True)
        o_ref[:, cols] = jnp.dot(p.astype(BF16), kv_ref[:, vcols],
                                 preferred_element_type=F32).astype(o_ref.dtype)


def _xa_attn(q, kv):
    tq = SEQ
    nq = SEQ // tq
    return pl.pallas_call(
        _xa_attn_kernel,
        grid=(BATCH, nq),
        in_specs=[pl.BlockSpec((tq, D_MODEL), lambda b, i: (b * nq + i, 0)),
                  pl.BlockSpec((MEM_LEN, 2 * D_MODEL), lambda b, i: (b, 0))],
        out_specs=pl.BlockSpec((tq, D_MODEL), lambda b, i: (b * nq + i, 0)),
        out_shape=jax.ShapeDtypeStruct((TOKENS, D_MODEL), BF16),
        compiler_params=_params("arbitrary", "arbitrary"),
        name="xa_attn",
    )(q, kv)


def _mla_weights(w_in, w_uq, w_ukv):
    pad = LANES - MLA_ROPE
    w_in = jnp.pad(w_in, ((0, 0), (0, pad)))
    w_uq = w_uq.reshape(MLA_Q_RANK, MLA_HEADS, MLA_NOPE + MLA_ROPE)
    w_uq = jnp.pad(w_uq, ((0, 0), (0, 0), (0, pad))).reshape(MLA_Q_RANK, MLA_HEADS * MLA_QK_PAD)
    w_ukv = w_ukv.reshape(MLA_KV_RANK, MLA_HEADS, MLA_NOPE + MLA_V)
    w_uk = w_ukv[:, :, :MLA_NOPE].reshape(MLA_KV_RANK, MLA_HEADS * MLA_NOPE)
    w_uv = w_ukv[:, :, MLA_NOPE:].reshape(MLA_KV_RANK, MLA_HEADS * MLA_V)
    return w_in.astype(BF16), w_uq.astype(BF16), w_uk.astype(BF16), w_uv.astype(BF16)


def kernel(x, mem, positions, ret_w_in, ret_gn_g, ret_w_out, mla_w_in, mla_q_norm_g, mla_w_uq, mla_kv_norm_g, mla_w_ukv, mla_w_out, xa_w_q, xa_w_kv, xa_w_out, ffn_w_up, ffn_conv_w, ffn_conv_b, ffn_w_down, ln_mix_g, ln_mix_b, ln_mem_g, ln_mem_b, ln_ffn_g, ln_ffn_b):
    h = x.reshape(TOKENS, D_MODEL)
    mem_b = mem.reshape(BATCH * MEM_LEN, D_MODEL).astype(BF16)
    cos_r, sin_r, c_m, sp_m, sn_m, hb = _rope_tables(positions, h)

    for layer in range(DEPTH):
        j = layer // N_MIXERS
        if layer % N_MIXERS == 0:
            qk = _ret_qk(hb, ret_w_in, j, cos_r, sin_r)
            vg, w_out = _xres_matmul(hb, ret_w_in, j, 2 * RET_QK_WIDTH, 2 * RET_V_WIDTH, tn=512,
                                     name="ret_vg", cast=(ret_w_out, j))
            mix_in = _ret_core(qk, vg, ret_gn_g[j])
        else:
            w_in, w_uq, w_uk, w_uv = _mla_weights(mla_w_in[j], mla_w_uq[j], mla_w_ukv[j])
            q, k, v, w_out = _mla_proj(hb, w_in, mla_q_norm_g[j].reshape(1, MLA_Q_RANK),
                                       mla_kv_norm_g[j].reshape(1, MLA_KV_RANK), w_uq, w_uk, w_uv,
                                       c_m, sp_m, sn_m, mla_w_out, j)
            mix_in = _mla_attn(q, k, v)
        h, hb = _mm_res_ln(mix_in, w_out, h, ln_mix_g[layer], ln_mix_b[layer], name="mix_out_ln")

        xq, xa_out_w = _xres_matmul(hb, xa_w_q, layer, 0, D_MODEL, tn=512, name="xa_q", cast=(xa_w_out, layer))
        xkv = _xres_matmul(mem_b, xa_w_kv, layer, 0, 2 * D_MODEL, tn=1024, name="xa_kv")
        xo = _xa_attn(xq, xkv)
        h, hb = _mm_res_ln(xo, xa_out_w, h, ln_mem_g[layer], ln_mem_b[layer], name="xa_out_ln")

        act, w_down = _ffn_up(hb, ffn_w_up, ffn_conv_w, ffn_conv_b, ffn_w_down, layer)
        h, hb = _mm_res_ln(act, w_down, h, ln_ffn_g[layer], ln_ffn_b[layer], name="ffn_down_ln")

    return h.reshape(BATCH, SEQ, D_MODEL)
```

```python
import functools

import jax
import jax.numpy as jnp
from jax import lax
from jax.experimental import pallas as pl
from jax.experimental.pallas import tpu as pltpu

D_MODEL = 2048
BATCH = 4
SEQ = 2048
DEPTH = 4
CHUNK = 64
MEM_LEN = 256
N_MIXERS = 2

RET_HEADS = 8
RET_QK_DIM = D_MODEL // RET_HEADS
RET_V_DIM = 2 * D_MODEL // RET_HEADS
RET_QK_WIDTH = RET_HEADS * RET_QK_DIM
RET_V_WIDTH = RET_HEADS * RET_V_DIM

MLA_HEADS = 16
MLA_Q_RANK = 512
MLA_KV_RANK = 512
MLA_NOPE = 128
MLA_ROPE = 64
MLA_V = 128

XA_HEADS = 4
XA_DIM = D_MODEL // XA_HEADS

D_FF = 5632
CONV_W = 3

ROPE_BASE = 10000.0
LN_EPS = 1e-5
RMS_EPS = 1e-6
NEG_INF = -1e30
DEEPNORM_ALPHA = (2 * DEPTH) ** 0.25
LOG2_E = 1.4426950408889634
MLA_Q_SCALE = (MLA_NOPE + MLA_ROPE) ** -0.5 * LOG2_E
MLA_HEADS_PER_STEP = 2

TOKENS = BATCH * SEQ
LANES = 128
BF16_SUBLANES = 16
MLA_QK_PAD = 256
RET_BLOCK = 256
V7X_VMEM_BYTES = 64 * 1024 * 1024
VMEM_LIMIT = V7X_VMEM_BYTES * 7 // 8
VMEM_LIMIT_LN = V7X_VMEM_BYTES * 15 // 16

X_ROWS = 2 * SEQ
ROW_CHUNK = 1024
W_BUFFERS = 3
FFN_ROW_CHUNK = 512
LN_SUB = 256

F32 = jnp.float32
BF16 = jnp.bfloat16


def _params(*semantics, vmem=VMEM_LIMIT):
    return pltpu.CompilerParams(dimension_semantics=semantics, vmem_limit_bytes=vmem)


def _single(shape, index_map):
    return pl.BlockSpec(shape, index_map, pipeline_mode=pl.Buffered(1))


def _rope_tables_kernel(pos_ref, invf_ret_ref, invf_mla_ref, x_ref, cos_r, sin_r, c_m, sp_m, sn_m, xb_ref):
    xb_ref[...] = x_ref[...].astype(BF16)
    pos = pos_ref[...].astype(F32)
    ang = pos * invf_ret_ref[...]
    cos_r[...] = jnp.cos(ang)
    sin_r[...] = jnp.sin(ang)
    angm = pos * invf_mla_ref[...]
    lane = lax.broadcasted_iota(jnp.int32, angm.shape, 1)
    half = MLA_ROPE // 2
    c = jnp.cos(angm)
    s = jnp.sin(angm)
    c_m[...] = jnp.where(lane < MLA_ROPE, c, 0.0)
    sp_m[...] = jnp.where((lane >= half) & (lane < MLA_ROPE), s, 0.0)
    sn_m[...] = jnp.where(lane < half, -s, 0.0)


def _rope_tables(positions, x):
    tm = 1024
    pos = positions.reshape(TOKENS, 1)
    invf_ret = ROPE_BASE ** (-jnp.arange(0, RET_QK_DIM, 2, dtype=F32) / RET_QK_DIM)
    invf_mla = ROPE_BASE ** (-jnp.arange(0, MLA_ROPE, 2, dtype=F32) / MLA_ROPE)
    invf_mla = jnp.concatenate([invf_mla, invf_mla, jnp.zeros((LANES - MLA_ROPE,), F32)])
    row = pl.BlockSpec((tm, LANES), lambda i: (i, 0))
    const = pl.BlockSpec((1, LANES), lambda i: (0, 0))
    wide = pl.BlockSpec((tm, D_MODEL), lambda i: (i, 0))
    return pl.pallas_call(
        _rope_tables_kernel,
        grid=(TOKENS // tm,),
        in_specs=[pl.BlockSpec((tm, 1), lambda i: (i, 0)), const, const, wide],
        out_specs=[row] * 5 + [wide],
        out_shape=[jax.ShapeDtypeStruct((TOKENS, LANES), F32)] * 5 + [jax.ShapeDtypeStruct((TOKENS, D_MODEL), BF16)],
        compiler_params=_params("arbitrary"),
        name="rope_tables",
    )(pos, invf_ret.reshape(1, LANES), invf_mla.reshape(1, LANES), x)


def _rope_mla(x, c, sp, sn):
    half = MLA_ROPE // 2
    return x * c + pltpu.roll(x, half, 1) * sp + pltpu.roll(x, LANES - half, 1) * sn


def _row_chunks(rows):
    chunk = min(ROW_CHUNK, rows)
    return [(r, chunk) for r in range(0, rows, chunk)]


def _cast_slab(src_ref, dst_ref):
    dst_ref[...] = src_ref[...].astype(BF16)


def _slab_specs(w_stack, layer, grid):
    k, n = w_stack.shape[1:]
    steps = 1
    for g in grid:
        steps *= g
    rows = k // steps
    assert rows * steps == k and rows % BF16_SUBLANES == 0
    if len(grid) == 1:
        flat = lambda i: i
    else:
        flat = lambda i, j: i * grid[1] + j
    src = pl.BlockSpec((None, rows, n), lambda *ids: (layer, flat(*ids), 0))
    dst = pl.BlockSpec((rows, n), lambda *ids: (flat(*ids), 0))
    return src, dst, jax.ShapeDtypeStruct((k, n), BF16)


def _xres_matmul_kernel(x_ref, w_ref, o_ref):
    w = w_ref[...].astype(BF16)
    for r, n in _row_chunks(x_ref.shape[0]):
        o_ref[r:r + n, :] = jnp.dot(x_ref[r:r + n, :], w, preferred_element_type=F32).astype(o_ref.dtype)


def _xres_matmul_cast_kernel(x_ref, w_ref, slab_ref, o_ref, slab_out_ref):
    _cast_slab(slab_ref, slab_out_ref)
    _xres_matmul_kernel(x_ref, w_ref, o_ref)


def _w_cols(w_stack, layer, col0, tn, buffers=2):
    k = w_stack.shape[1]
    first = col0 // tn
    return pl.BlockSpec((None, k, tn), lambda i, j: (layer, 0, first + j), pipeline_mode=pl.Buffered(buffers))


def _xres_matmul(x, w_stack, layer, col0, n, tn, name, cast=None):
    m, k = x.shape
    rows = min(X_ROWS, m)
    nj = n // tn
    first = col0 // tn
    any_spec = pl.BlockSpec(memory_space=pl.ANY)
    out_shape = [jax.ShapeDtypeStruct((m, n), BF16)]
    args = [x, w_stack]
    if cast is not None:
        k2, n2 = cast[0].shape[1:]
        slab = k2 // ((m // rows) * nj)
        assert slab * (m // rows) * nj == k2 and slab % BF16_SUBLANES == 0
        out_shape.append(jax.ShapeDtypeStruct((k2, n2), BF16))
        args.append(cast[0])

    def outer(x_ref, *hbm):
        i = pl.program_id(0)
        in_specs = [pl.BlockSpec((None, k, tn), lambda j: (layer, 0, first + j),
                                 pipeline_mode=pl.Buffered(W_BUFFERS))]
        out_specs = [pl.BlockSpec((rows, tn), lambda j: (i, j))]
        if cast is None:
            body = lambda w_ref, o_ref: _xres_matmul_kernel(x_ref, w_ref, o_ref)
        else:
            in_specs.append(pl.BlockSpec((None, slab, n2), lambda j: (cast[1], i * nj + j, 0)))
            out_specs.append(pl.BlockSpec((slab, n2), lambda j: (i * nj + j, 0)))
            body = lambda w_ref, s_ref, o_ref, so_ref: _xres_matmul_cast_kernel(x_ref, w_ref, s_ref, o_ref, so_ref)
        pltpu.emit_pipeline(body, grid=(nj,), in_specs=in_specs, out_specs=out_specs)(*hbm)

    out = pl.pallas_call(
        outer,
        grid=(m // rows,),
        in_specs=[_single((rows, k), lambda i: (i, 0))] + [any_spec] * (len(args) - 1),
        out_specs=[any_spec] * len(out_shape),
        out_shape=out_shape,
        compiler_params=_params("arbitrary"),
        name=name,
    )(*args)
    return out if cast is not None else out[0]


def _ret_qk_kernel(tn, x_ref, w_ref, cos_ref, sin_ref, o_ref):
    j = pl.program_id(1)
    w = w_ref[...].astype(BF16)
    scale = jnp.where(j >= RET_QK_WIDTH // tn, RET_QK_DIM ** -0.5, 1.0).astype(F32)
    half = RET_QK_DIM // 2
    for r, n in _row_chunks(x_ref.shape[0]):
        acc = jnp.dot(x_ref[r:r + n, :], w, preferred_element_type=F32)
        cos = cos_ref[r:r + n, :]
        sin = sin_ref[r:r + n, :]
        for lo in range(0, tn, RET_QK_DIM):
            x1 = acc[:, lo:lo + half]
            x2 = acc[:, lo + half:lo + RET_QK_DIM]
            o_ref[r:r + n, lo:lo + half] = ((x1 * cos - x2 * sin) * scale).astype(o_ref.dtype)
            o_ref[r:r + n, lo + half:lo + RET_QK_DIM] = ((x2 * cos + x1 * sin) * scale).astype(o_ref.dtype)


def _ret_qk(xb, w_stack, layer, cos, sin):
    tn = 512
    n = 2 * RET_QK_WIDTH
    rows = X_ROWS
    return pl.pallas_call(
        functools.partial(_ret_qk_kernel, tn),
        grid=(TOKENS // rows, n // tn),
        in_specs=[_single((rows, D_MODEL), lambda i, j: (i, 0)),
                  _w_cols(w_stack, layer, 0, tn),
                  _single((rows, LANES), lambda i, j: (i, 0)),
                  _single((rows, LANES), lambda i, j: (i, 0))],
        out_specs=pl.BlockSpec((rows, tn), lambda i, j: (i, j)),
        out_shape=jax.ShapeDtypeStruct((TOKENS, n), BF16),
        compiler_params=_params("arbitrary", "arbitrary"),
        name="ret_qk",
    )(xb, w_stack, cos, sin)


def _ffn_up_kernel(x_ref, wg_ref, wv_ref, cwg_ref, cwv_ref, cbg_ref, cbv_ref, slab_ref,
                   o_ref, slab_out_ref, hg_ref, hv_ref):
    _cast_slab(slab_ref, slab_out_ref)
    halo = BF16_SUBLANES
    wg = wg_ref[...].astype(BF16)
    wv = wv_ref[...].astype(BF16)
    for r in range(0, x_ref.shape[0], FFN_ROW_CHUNK):
        n = FFN_ROW_CHUNK
        seq_start = r % SEQ == 0

        def conv(w, cw_ref, cb_ref, h_ref):
            if seq_start:
                h_ref[0:halo, :] = jnp.zeros((halo, h_ref.shape[1]), F32)
                h_ref[halo:, :] = jnp.dot(x_ref[r:r + n, :], w, preferred_element_type=F32)
            else:
                h_ref[...] = jnp.dot(x_ref[r - halo:r + n, :], w, preferred_element_type=F32)
            cw = cw_ref[...]
            out = (cw[2:3] * h_ref[halo:halo + n, :] + cw[1:2] * h_ref[halo - 1:halo - 1 + n, :]
                   + cw[0:1] * h_ref[halo - 2:halo - 2 + n, :])
            return out + cb_ref[...]

        gate = conv(wg, cwg_ref, cbg_ref, hg_ref)
        val = conv(wv, cwv_ref, cbv_ref, hv_ref)
        o_ref[r:r + n, :] = (gate * jax.nn.sigmoid(gate) * val).astype(o_ref.dtype)


def _ffn_up(xb, w_up, conv_w, conv_b, w_down, layer):
    tn = 512
    nj = D_FF // tn
    rows = X_ROWS
    grid = (TOKENS // rows, nj)
    conv_b = conv_b.reshape(DEPTH, 1, 2 * D_FF)
    taps = lambda depth, first: pl.BlockSpec((None, depth, tn), lambda i, j: (layer, 0, first + j))
    slab_src, slab_dst, slab_shape = _slab_specs(w_down, layer, grid)
    return pl.pallas_call(
        _ffn_up_kernel,
        grid=grid,
        in_specs=[_single((rows, D_MODEL), lambda i, j: (i, 0)),
                  _w_cols(w_up, layer, 0, tn), _w_cols(w_up, layer, D_FF, tn),
                  taps(CONV_W, 0), taps(CONV_W, nj), taps(1, 0), taps(1, nj), slab_src],
        out_specs=[pl.BlockSpec((rows, tn), lambda i, j: (i, j)), slab_dst],
        out_shape=[jax.ShapeDtypeStruct((TOKENS, D_FF), BF16), slab_shape],
        scratch_shapes=[pltpu.VMEM((BF16_SUBLANES + FFN_ROW_CHUNK, tn), F32)] * 2,
        compiler_params=_params("arbitrary", "arbitrary"),
        name="ffn_up",
    )(xb, w_up, w_up, conv_w, conv_w, conv_b, conv_b, w_down)


def _mm_res_ln_kernel(a_ref, w_ref, h_ref, g_ref, b_ref, of_ref, ob_ref):
    g = g_ref[...]
    b = b_ref[...]
    for r in range(0, a_ref.shape[0], LN_SUB):
        rows = slice(r, r + LN_SUB)
        y = DEEPNORM_ALPHA * h_ref[rows, :] + jnp.dot(a_ref[rows, :], w_ref[...], preferred_element_type=F32)
        mu = jnp.mean(y, -1, keepdims=True)
        d = y - mu
        var = jnp.mean(d * d, -1, keepdims=True)
        out = d * lax.rsqrt(var + LN_EPS) * g + b
        of_ref[rows, :] = out
        ob_ref[rows, :] = out.astype(BF16)


def _mm_res_ln(a, w, h, g, b, name):
    m, k = a.shape
    n = w.shape[1]
    tm = 512
    row = lambda width: pl.BlockSpec((tm, width), lambda i: (i, 0))
    vec = pl.BlockSpec((1, n), lambda i: (0, 0))
    return pl.pallas_call(
        _mm_res_ln_kernel,
        grid=(m // tm,),
        in_specs=[row(k), _single((k, n), lambda i: (0, 0)), row(n), vec, vec],
        out_specs=[row(n), row(n)],
        out_shape=[jax.ShapeDtypeStruct((m, n), F32), jax.ShapeDtypeStruct((m, n), BF16)],
        compiler_params=_params("arbitrary", vmem=VMEM_LIMIT_LN if k > RET_V_WIDTH else VMEM_LIMIT),
        name=name,
    )(a, w, h, g.reshape(1, n), b.reshape(1, n))


def _ret_core_kernel(lg_ref, q_ref, k_ref, v_ref, g_ref, gn_ref, y_ref, state_ref, decay_ref):
    L = RET_BLOCK
    head = pl.program_id(1)
    step = pl.program_id(2)
    lg = lg_ref[head]

    @pl.when(step == 0)
    def _():
        state_ref[...] = jnp.zeros_like(state_ref)
        n = lax.broadcasted_iota(jnp.int32, (L, L), 0)
        m = lax.broadcasted_iota(jnp.int32, (L, L), 1)
        dist = jnp.abs(n - m).astype(F32)
        decay_ref[...] = jnp.where((m // CHUNK) <= (n // CHUNK), jnp.exp(lg * dist), 0.0)

    idx = lax.broadcasted_iota(jnp.int32, (L, 1), 0).astype(F32)
    q_decay = jnp.exp(idx * lg)
    k_decay = jnp.exp((L - idx) * lg)
    block_decay = jnp.exp(jnp.full((1, RET_V_DIM), L, F32) * lg)
    decay = decay_ref[...]
    gn = gn_ref[...]
    state = state_ref[...]
    for r in range(0, q_ref.shape[0], L):
        rows = slice(r, r + L)
        q = q_ref[rows, :]
        k = k_ref[rows, :]
        v = v_ref[rows, :]
        s = lax.dot_general(q, k, (((1,), (1,)), ((), ())), preferred_element_type=F32)
        o = jnp.dot((s * decay).astype(BF16), v, preferred_element_type=F32)
        o = o + jnp.dot(q, state.astype(BF16), preferred_element_type=F32) * q_decay
        k_dec = (k.astype(F32) * k_decay).astype(BF16)
        upd = lax.dot_general(k_dec, v, (((0,), (0,)), ((), ())), preferred_element_type=F32)
        state = state * block_decay + upd

        mu = jnp.mean(o, -1, keepdims=True)
        d = o - mu
        var = jnp.mean(d * d, -1, keepdims=True)
        normed = d * lax.rsqrt(var + LN_EPS) * gn
        gate = g_ref[rows, :].astype(F32)
        y_ref[rows, :] = (gate * jax.nn.sigmoid(gate) * normed).astype(y_ref.dtype)
    state_ref[...] = state


def _ret_core(qk, vg, gn_g):
    ts = SEQ
    ns = SEQ // ts
    log_gamma = jnp.log(1.0 - 2.0 ** (-5.0 - jnp.arange(RET_HEADS, dtype=F32)))
    return pl.pallas_call(
        _ret_core_kernel,
        grid=(BATCH, RET_HEADS, ns),
        in_specs=[pl.BlockSpec(memory_space=pltpu.SMEM),
                  pl.BlockSpec((ts, RET_QK_DIM), lambda b, h, c: (b * ns + c, h)),
                  pl.BlockSpec((ts, RET_QK_DIM), lambda b, h, c: (b * ns + c, RET_HEADS + h)),
                  pl.BlockSpec((ts, RET_V_DIM), lambda b, h, c: (b * ns + c, h)),
                  pl.BlockSpec((ts, RET_V_DIM), lambda b, h, c: (b * ns + c, RET_HEADS + h)),
                  pl.BlockSpec((1, RET_V_DIM), lambda b, h, c: (0, h))],
        out_specs=pl.BlockSpec((ts, RET_V_DIM), lambda b, h, c: (b * ns + c, h)),
        out_shape=jax.ShapeDtypeStruct((TOKENS, RET_V_WIDTH), BF16),
        scratch_shapes=[pltpu.VMEM((RET_QK_DIM, RET_V_DIM), F32),
                        pltpu.VMEM((RET_BLOCK, RET_BLOCK), F32)],
        compiler_params=_params("arbitrary", "arbitrary", "arbitrary"),
        name="ret_core",
    )(log_gamma, qk, qk, vg, vg, gn_g.reshape(1, RET_V_WIDTH))


def _rms(x, g):
    return x * lax.rsqrt(jnp.mean(x * x, -1, keepdims=True) + RMS_EPS) * g


def _mla_proj_kernel(x_ref, w_in_ref, qg_ref, kvg_ref, w_uq_ref, w_uk_ref, w_uv_ref,
                     c_ref, sp_ref, sn_ref, slab_ref, q_ref, k_ref, v_ref, slab_out_ref):
    _cast_slab(slab_ref, slab_out_ref)
    c = c_ref[...]
    sp = sp_ref[...]
    sn = sn_ref[...]
    proj = jnp.dot(x_ref[...], w_in_ref[...], preferred_element_type=F32)
    c_q = _rms(proj[:, :MLA_Q_RANK], qg_ref[...]).astype(BF16)
    c_kv = _rms(proj[:, MLA_Q_RANK:MLA_Q_RANK + MLA_KV_RANK], kvg_ref[...]).astype(BF16)
    k_rope = _rope_mla(proj[:, MLA_Q_RANK + MLA_KV_RANK:], c, sp, sn).astype(BF16)

    q = jnp.dot(c_q, w_uq_ref[...], preferred_element_type=F32)
    k_nope = jnp.dot(c_kv, w_uk_ref[...], preferred_element_type=F32)
    for hh in range(MLA_HEADS):
        lo = hh * MLA_QK_PAD
        q_ref[:, lo:lo + MLA_NOPE] = (q[:, lo:lo + MLA_NOPE] * MLA_Q_SCALE).astype(BF16)
        q_ref[:, lo + MLA_NOPE:lo + MLA_QK_PAD] = (_rope_mla(
            q[:, lo + MLA_NOPE:lo + MLA_QK_PAD], c, sp, sn) * MLA_Q_SCALE).astype(BF16)
        k_ref[:, lo:lo + MLA_NOPE] = k_nope[:, hh * MLA_NOPE:(hh + 1) * MLA_NOPE].astype(BF16)
        k_ref[:, lo + MLA_NOPE:lo + MLA_QK_PAD] = k_rope
    v_ref[...] = jnp.dot(c_kv, w_uv_ref[...], preferred_element_type=F32).astype(BF16)


def _mla_proj(xb, w_in, qg, kvg, w_uq, w_uk, w_uv, c, sp, sn, w_out, layer):
    tm = 512
    qk_w = MLA_HEADS * MLA_QK_PAD
    v_w = MLA_HEADS * MLA_V
    grid = (TOKENS // tm,)
    row = lambda w: pl.BlockSpec((tm, w), lambda i: (i, 0))
    res = lambda a: _single(a.shape, lambda i: (0, 0))
    slab_src, slab_dst, slab_shape = _slab_specs(w_out, layer, grid)
    return pl.pallas_call(
        _mla_proj_kernel,
        grid=grid,
        in_specs=[row(D_MODEL), res(w_in), res(qg), res(kvg), res(w_uq), res(w_uk), res(w_uv),
                  row(LANES), row(LANES), row(LANES), slab_src],
        out_specs=[row(qk_w), row(qk_w), row(v_w), slab_dst],
        out_shape=[jax.ShapeDtypeStruct((TOKENS, qk_w), BF16),
                   jax.ShapeDtypeStruct((TOKENS, qk_w), BF16),
                   jax.ShapeDtypeStruct((TOKENS, v_w), BF16), slab_shape],
        compiler_params=_params("arbitrary"),
        name="mla_proj",
    )(xb, w_in, qg, kvg, w_uq, w_uk, w_uv, c, sp, sn, w_out)


def _mla_attn_kernel(tq, q_ref, k_ref, v_ref, o_ref, s_ref):
    dims = (((1,), (1,)), ((), ()))
    row = lax.broadcasted_iota(jnp.int32, (tq, tq), 0)
    col = lax.broadcasted_iota(jnp.int32, (tq, tq), 1)
    visible = (row // CHUNK) >= (col // CHUNK)
    for c in range(SEQ // tq):
        lo = c * tq
        n = lo + tq
        for hh in range(MLA_HEADS_PER_STEP):
            qk = slice(hh * MLA_QK_PAD, (hh + 1) * MLA_QK_PAD)
            vc = slice(hh * MLA_V, (hh + 1) * MLA_V)
            buf = s_ref.at[MLA_HEADS_PER_STEP * (c % 2) + hh]
            buf[:, 0:n] = lax.dot_general(q_ref[lo:n, qk], k_ref[0:n, qk], dims, preferred_element_type=F32)
            buf[:, lo:n] = jnp.where(visible, buf[:, lo:n], NEG_INF)
            s = buf[:, 0:n]
            p = jnp.exp2(s - jnp.max(s, -1, keepdims=True))
            l = jnp.sum(p, -1, keepdims=True)
            acc = jnp.dot(p.astype(BF16), v_ref[0:n, vc], preferred_element_type=F32)
            o_ref[lo:n, vc] = (acc * (1.0 / l)).astype(o_ref.dtype)


def _mla_attn(q, k, v):
    tq = 256
    hp = MLA_HEADS_PER_STEP
    return pl.pallas_call(
        functools.partial(_mla_attn_kernel, tq),
        grid=(BATCH, MLA_HEADS // hp),
        in_specs=[pl.BlockSpec((SEQ, hp * MLA_QK_PAD), lambda b, h: (b, h)),
                  pl.BlockSpec((SEQ, hp * MLA_QK_PAD), lambda b, h: (b, h)),
                  pl.BlockSpec((SEQ, hp * MLA_V), lambda b, h: (b, h))],
        out_specs=pl.BlockSpec((SEQ, hp * MLA_V), lambda b, h: (b, h)),
        out_shape=jax.ShapeDtypeStruct((TOKENS, MLA_HEADS * MLA_V), BF16),
        scratch_shapes=[pltpu.VMEM((2 * hp, tq, SEQ), F32)],
        compiler_params=_params("arbitrary", "arbitrary"),
        name="mla_attn",
    )(q, k, v)


def _xa_attn_kernel(q_ref, kv_ref, o_ref):
    for hh in range(XA_HEADS):
        cols = slice(hh * XA_DIM, (hh + 1) * XA_DIM)
        vcols = slice(D_MODEL + hh * XA_DIM, D_MODEL + (hh + 1) * XA_DIM)
        s = lax.dot_general(q_ref[:, cols], kv_ref[:, cols], (((1,), (1,)), ((), ())),
                            preferred_element_type=F32) * (XA_DIM ** -0.5)
        e = jnp.exp(s - jnp.max(s, -1, keepdims=True))
        p = e / jnp.sum(e, -1, keepdims=True)
        o_ref[:, cols] = jnp.dot(p.astype(BF16), kv_ref[:, vcols],
                                 preferred_element_type=F32).astype(o_ref.dtype)


def _xa_attn(q, kv):
    tq = 1024
    nq = SEQ // tq
    return pl.pallas_call(
        _xa_attn_kernel,
        grid=(BATCH, nq),
        in_specs=[pl.BlockSpec((tq, D_MODEL), lambda b, i: (b * nq + i, 0)),
                  pl.BlockSpec((MEM_LEN, 2 * D_MODEL), lambda b, i: (b, 0))],
        out_specs=pl.BlockSpec((tq, D_MODEL), lambda b, i: (b * nq + i, 0)),
        out_shape=jax.ShapeDtypeStruct((TOKENS, D_MODEL), BF16),
        compiler_params=_params("arbitrary", "arbitrary"),
        name="xa_attn",
    )(q, kv)


def _mla_weights(w_in, w_uq, w_ukv):
    pad = LANES - MLA_ROPE
    w_in = jnp.pad(w_in, ((0, 0), (0, pad)))
    w_uq = w_uq.reshape(MLA_Q_RANK, MLA_HEADS, MLA_NOPE + MLA_ROPE)
    w_uq = jnp.pad(w_uq, ((0, 0), (0, 0), (0, pad))).reshape(MLA_Q_RANK, MLA_HEADS * MLA_QK_PAD)
    w_ukv = w_ukv.reshape(MLA_KV_RANK, MLA_HEADS, MLA_NOPE + MLA_V)
    w_uk = w_ukv[:, :, :MLA_NOPE].reshape(MLA_KV_RANK, MLA_HEADS * MLA_NOPE)
    w_uv = w_ukv[:, :, MLA_NOPE:].reshape(MLA_KV_RANK, MLA_HEADS * MLA_V)
    return w_in.astype(BF16), w_uq.astype(BF16), w_uk.astype(BF16), w_uv.astype(BF16)


def kernel(x, mem, positions, ret_w_in, ret_gn_g, ret_w_out, mla_w_in, mla_q_norm_g, mla_w_uq, mla_kv_norm_g, mla_w_ukv, mla_w_out, xa_w_q, xa_w_kv, xa_w_out, ffn_w_up, ffn_conv_w, ffn_conv_b, ffn_w_down, ln_mix_g, ln_mix_b, ln_mem_g, ln_mem_b, ln_ffn_g, ln_ffn_b):
    h = x.reshape(TOKENS, D_MODEL)
    mem_b = mem.reshape(BATCH * MEM_LEN, D_MODEL).astype(BF16)
    cos_r, sin_r, c_m, sp_m, sn_m, hb = _rope_tables(positions, h)

    for layer in range(DEPTH):
        j = layer // N_MIXERS
        if layer % N_MIXERS == 0:
            qk = _ret_qk(hb, ret_w_in, j, cos_r, sin_r)
            vg, w_out = _xres_matmul(hb, ret_w_in, j, 2 * RET_QK_WIDTH, 2 * RET_V_WIDTH, tn=512,
                                     name="ret_vg", cast=(ret_w_out, j))
            mix_in = _ret_core(qk, vg, ret_gn_g[j])
        else:
            w_in, w_uq, w_uk, w_uv = _mla_weights(mla_w_in[j], mla_w_uq[j], mla_w_ukv[j])
            q, k, v, w_out = _mla_proj(hb, w_in, mla_q_norm_g[j].reshape(1, MLA_Q_RANK),
                                       mla_kv_norm_g[j].reshape(1, MLA_KV_RANK), w_uq, w_uk, w_uv,
                                       c_m, sp_m, sn_m, mla_w_out, j)
            mix_in = _mla_attn(q, k, v)
        h, hb = _mm_res_ln(mix_in, w_out, h, ln_mix_g[layer], ln_mix_b[layer], name="mix_out_ln")

        xq, xa_out_w = _xres_matmul(hb, xa_w_q, layer, 0, D_MODEL, tn=512, name="xa_q", cast=(xa_w_out, layer))
        xkv = _xres_matmul(mem_b, xa_w_kv, layer, 0, 2 * D_MODEL, tn=512, name="xa_kv")
        xo = _xa_attn(xq, xkv)
        h, hb = _mm_res_ln(xo, xa_out_w, h, ln_mem_g[layer], ln_mem_b[layer], name="xa_out_ln")

        act, w_down = _ffn_up(hb, ffn_w_up, ffn_conv_w, ffn_conv_b, ffn_w_down, layer)
        h, hb = _mm_res_ln(act, w_down, h, ln_ffn_g[layer], ln_ffn_b[layer], name="ffn_down_ln")

    return h.reshape(BATCH, SEQ, D_MODEL)
```
